```python
import jax, jax.numpy as jnp
from jax import lax
import numpy as np

D_MODEL = 1024
BATCH = 32
SEQ = 2048
DEPTH = 1

HEAD_DIM = 64
MOBA_HEADS = 8
MOBA_BLOCK = 256
MOBA_TOPK = 3
MOBA_Q_CHUNK = 16
SWA_Q_HEADS = 8
SWA_KV_HEADS = 2
SWA_WINDOW = 128
MEM_LEN = 256
XATTN_HEADS = 4
XATTN_HEAD_DIM = 128
N_GROUPS = 4
EXPERTS_PER_GROUP = 8
EXPERT_TOPK = 2
D_EXPERT = 256

RMS_EPS = 1e-6
NEG = -1e30

MOBA_WIDTH = MOBA_HEADS * HEAD_DIM
SWA_Q_WIDTH = SWA_Q_HEADS * HEAD_DIM
SWA_KV_WIDTH = SWA_KV_HEADS * HEAD_DIM
IN_SPLITS = (MOBA_WIDTH, MOBA_WIDTH, MOBA_WIDTH,
             SWA_Q_WIDTH, SWA_KV_WIDTH, SWA_KV_WIDTH,
             D_MODEL, D_MODEL)
IN_WIDTH = sum(IN_SPLITS)

kernel_name = "hybrid_moba_swa_gated_hiermoe"


def rmsnorm(x, g):
    xf = x.astype(jnp.float32)
    y = xf * lax.rsqrt(jnp.mean(xf * xf, axis=-1, keepdims=True) + RMS_EPS)
    return (y * g.astype(jnp.float32)).astype(x.dtype)


def alibi_slopes(n_heads):
    return jnp.asarray([2.0 ** (-8.0 * (i + 1) / n_heads) for i in range(n_heads)], dtype=jnp.float32)


def moba_attention(q, k, v, slopes):
    B, T, H, hd = q.shape
    nb = -(-T // MOBA_BLOCK)
    tp = nb * MOBA_BLOCK
    pad = tp - T
    q, k, v = [jnp.pad(a, ((0, 0), (0, pad), (0, 0), (0, 0))).transpose(0, 2, 1, 3) for a in (q, k, v)]
    kb = k.reshape(B, H, nb, MOBA_BLOCK, hd)
    vb = v.reshape(B, H, nb, MOBA_BLOCK, hd)
    kmean = jnp.mean(kb.astype(jnp.float32), axis=3)
    ksel = min(MOBA_TOPK, nb)
    n_chunks = tp // MOBA_Q_CHUNK
    qc = q.reshape(B, H, n_chunks, MOBA_Q_CHUNK, hd).transpose(2, 0, 1, 3, 4)
    scale = hd ** -0.5
    bi = jnp.arange(B)[:, None, None, None]
    hi = jnp.arange(H)[None, :, None, None]
    blk_ids = jnp.arange(nb)
    offs = jnp.arange(MOBA_BLOCK)

    def chunk(args):
        q_c, c = args
        t = c * MOBA_Q_CHUNK + jnp.arange(MOBA_Q_CHUNK)
        qblk = (c * MOBA_Q_CHUNK) // MOBA_BLOCK
        gate = jnp.einsum('bhqd,bhnd->bhqn', q_c.astype(jnp.float32), kmean)
        gate = jnp.where(blk_ids < qblk, gate, NEG)
        _, idx = lax.top_k(gate, ksel)
        valid = idx < qblk
        k_sel = kb[bi, hi, idx]
        v_sel = vb[bi, hi, idx]
        s_sel = jnp.einsum('bhqd,bhqksd->bhqks', q_c, k_sel, preferred_element_type=jnp.float32) * scale
        pos_sel = idx[..., None] * MOBA_BLOCK + offs
        dist_sel = (t[:, None, None] - pos_sel).astype(jnp.float32)
        s_sel = s_sel - slopes[:, None, None, None] * dist_sel
        s_sel = jnp.where(valid[..., None], s_sel, NEG)
        k_own = lax.dynamic_slice_in_dim(k, qblk * MOBA_BLOCK, MOBA_BLOCK, axis=2)
        v_own = lax.dynamic_slice_in_dim(v, qblk * MOBA_BLOCK, MOBA_BLOCK, axis=2)
        s_own = jnp.einsum('bhqd,bhsd->bhqs', q_c, k_own, preferred_element_type=jnp.float32) * scale
        dist_own = t[:, None] - (qblk * MOBA_BLOCK + offs)[None, :]
        s_own = s_own - slopes[:, None, None] * dist_own.astype(jnp.float32)
        s_own = jnp.where(dist_own >= 0, s_own, NEG)
        s = jnp.concatenate([s_sel.reshape(B, H, MOBA_Q_CHUNK, ksel * MOBA_BLOCK), s_own], axis=-1)
        p = jax.nn.softmax(s, axis=-1).astype(v.dtype)
        p_sel = p[..., :ksel * MOBA_BLOCK].reshape(B, H, MOBA_Q_CHUNK, ksel, MOBA_BLOCK)
        p_own = p[..., ksel * MOBA_BLOCK:]
        return (jnp.einsum('bhqks,bhqksd->bhqd', p_sel, v_sel)
                + jnp.einsum('bhqs,bhsd->bhqd', p_own, v_own))

    o = lax.map(chunk, (qc, jnp.arange(n_chunks)))
    o = o.transpose(1, 0, 3, 2, 4).reshape(B, tp, H * hd)
    return o[:, :T]


def swa_attention(q, k, v, sinks, slopes):
    B, T, HQ, hd = q.shape
    HKV = k.shape[2]
    G = HQ // HKV
    W = SWA_WINDOW
    nw = T // W
    qb = q.reshape(B, nw, W, HKV, G, hd)
    kb = k.reshape(B, nw, W, HKV, hd)
    vb = v.reshape(B, nw, W, HKV, hd)
    shift = lambda a: jnp.concatenate([jnp.zeros_like(a[:, :1]), a[:, :-1]], axis=1)
    kk = jnp.concatenate([shift(kb), kb], axis=2)
    vv = jnp.concatenate([shift(vb), vb], axis=2)
    s = jnp.einsum('bnqkgd,bnskd->bnkgqs', qb, kk, preferred_element_type=jnp.float32) * (hd ** -0.5)
    qi = jnp.arange(W)
    si = jnp.arange(2 * W)
    rel = qi[:, None] - si[None, :] + W
    s_abs = jnp.arange(nw)[:, None] * W - W + si[None, :]
    valid = ((rel >= 0) & (rel < W))[None] & (s_abs >= 0)[:, None, :]
    s = s - slopes.reshape(HKV, G)[:, :, None, None] * rel.astype(jnp.float32)
    s = jnp.where(valid[None, :, None, None], s, NEG)
    sink = jnp.broadcast_to(sinks.astype(jnp.float32).reshape(1, 1, HKV, G, 1, 1), s.shape[:-1] + (1,))
    p = jax.nn.softmax(jnp.concatenate([s, sink], axis=-1), axis=-1)[..., :2 * W].astype(v.dtype)
    o = jnp.einsum('bnkgqs,bnskd->bnqkgd', p, vv)
    return o.reshape(B, T, HQ * hd)


def cross_attention(h, memn, w_q, w_kv, w_o):
    B, T, _ = h.shape
    M = memn.shape[1]
    q = (h @ w_q).reshape(B, T, XATTN_HEADS, XATTN_HEAD_DIM)
    k, v = jnp.split((memn @ w_kv).reshape(B, M, 2 * XATTN_HEADS, XATTN_HEAD_DIM), 2, axis=2)
    s = jnp.einsum('bthd,bmhd->bhtm', q, k, preferred_element_type=jnp.float32) * (XATTN_HEAD_DIM ** -0.5)
    p = jax.nn.softmax(s, axis=-1).astype(v.dtype)
    o = jnp.einsum('bhtm,bmhd->bthd', p, v).reshape(B, T, XATTN_HEADS * XATTN_HEAD_DIM)
    return o @ w_o


def hier_moe(h, w_grp, b_grp, w_exp, b_exp, w_gate, w_up, w_down):
    B, T, D = h.shape
    n = B * T
    hf = h.reshape(n, D)
    g_prob = jax.nn.softmax((hf @ w_grp).astype(jnp.float32) + b_grp.astype(jnp.float32), axis=-1)
    g_p, g_idx = lax.top_k(g_prob, 1)
    e_logits = ((hf @ w_exp).astype(jnp.float32) + b_exp.astype(jnp.float32)).reshape(n, N_GROUPS, EXPERTS_PER_GROUP)
    e_logits = jnp.take_along_axis(e_logits, g_idx[:, :, None], axis=1)[:, 0]
    e_p, e_idx = lax.top_k(jax.nn.softmax(e_logits, axis=-1), EXPERT_TOPK)
    e_w = e_p / jnp.sum(e_p, axis=-1, keepdims=True) * g_p
    w_in_group = jnp.sum(jax.nn.one_hot(e_idx, EXPERTS_PER_GROUP, dtype=jnp.float32) * e_w[..., None], axis=1)
    combine = (jax.nn.one_hot(g_idx[:, 0], N_GROUPS, dtype=jnp.float32)[:, :, None]
               * w_in_group[:, None, :]).astype(h.dtype)
    out = jnp.zeros((n, D), h.dtype)
    for g in range(N_GROUPS):
        hid = (jax.nn.silu(jnp.einsum('nd,edf->nef', hf, w_gate[g]))
               * jnp.einsum('nd,edf->nef', hf, w_up[g]) * combine[:, g, :, None])
        out = out + jnp.einsum('nef,efd->nd', hid, w_down[g])
    return out.reshape(B, T, D)


def setup_inputs(seed: int = 0) -> dict:
    key = jax.random.key(seed)
    ks = jax.random.split(key, 24)
    L, D = DEPTH, D_MODEL
    f32 = jnp.float32
    nrm = lambda k, shape, fan_in: jax.random.normal(k, shape, f32) * (fan_in ** -0.5)
    gain = lambda k, shape: 1.0 + 0.02 * jax.random.normal(k, shape, f32)
    return {
        "x": jax.random.normal(ks[0], (BATCH, SEQ, D), f32),
        "mem": jax.random.normal(ks[1], (BATCH, MEM_LEN, D), f32),
        "mix_norm_g": gain(ks[2], (L, D)),
        "w_in": nrm(ks[3], (L, D, IN_WIDTH), D),
        "moba_out_w": nrm(ks[4], (L, MOBA_WIDTH, D), MOBA_WIDTH),
        "swa_out_w": nrm(ks[5], (L, SWA_Q_WIDTH, D), SWA_Q_WIDTH),
        "swa_sinks": 0.5 * jax.random.normal(ks[6], (L, SWA_Q_HEADS), f32),
        "mix_out_w": nrm(ks[7], (L, D, D), D),
        "xattn_norm_g": gain(ks[8], (L, D)),
        "mem_norm_g": gain(ks[9], (L, D)),
        "xattn_wq": nrm(ks[10], (L, D, XATTN_HEADS * XATTN_HEAD_DIM), D),
        "xattn_wkv": nrm(ks[11], (L, D, 2 * XATTN_HEADS * XATTN_HEAD_DIM), D),
        "xattn_wo": nrm(ks[12], (L, XATTN_HEADS * XATTN_HEAD_DIM, D), XATTN_HEADS * XATTN_HEAD_DIM),
        "ffn_norm_g": gain(ks[13], (L, D)),
        "router_group_w": nrm(ks[14], (L, D, N_GROUPS), D),
        "router_group_b": 0.01 * jax.random.normal(ks[15], (L, N_GROUPS), f32),
        "router_expert_w": nrm(ks[16], (L, D, N_GROUPS * EXPERTS_PER_GROUP), D),
        "router_expert_b": 0.01 * jax.random.normal(ks[17], (L, N_GROUPS * EXPERTS_PER_GROUP), f32),
        "expert_w_gate": nrm(ks[18], (L, N_GROUPS, EXPERTS_PER_GROUP, D, D_EXPERT), D),
        "expert_w_up": nrm(ks[19], (L, N_GROUPS, EXPERTS_PER_GROUP, D, D_EXPERT), D),
        "expert_w_down": nrm(ks[20], (L, N_GROUPS, EXPERTS_PER_GROUP, D_EXPERT, D), D_EXPERT),
        "final_norm_g": gain(ks[21], (D,)),
    }


def reference(x, mem, mix_norm_g, w_in, moba_out_w, swa_out_w, swa_sinks, mix_out_w,
              xattn_norm_g, mem_norm_g, xattn_wq, xattn_wkv, xattn_wo, ffn_norm_g,
              router_group_w, router_group_b, router_expert_w, router_expert_b,
              expert_w_gate, expert_w_up, expert_w_down, final_norm_g):
    B, T, _ = x.shape
    split_at = [int(v) for v in np.cumsum(IN_SPLITS)[:-1]]
    moba_slopes = alibi_slopes(MOBA_HEADS)
    swa_slopes = alibi_slopes(SWA_Q_HEADS)
    for l in range(DEPTH):
        h = rmsnorm(x, mix_norm_g[l])
        q_a, k_a, v_a, q_b, k_b, v_b, g_a, g_b = jnp.split(h @ w_in[l], split_at, axis=-1)
        a = moba_attention(q_a.reshape(B, T, MOBA_HEADS, HEAD_DIM),
                           k_a.reshape(B, T, MOBA_HEADS, HEAD_DIM),
                           v_a.reshape(B, T, MOBA_HEADS, HEAD_DIM), moba_slopes)
        b = swa_attention(q_b.reshape(B, T, SWA_Q_HEADS, HEAD_DIM),
                          k_b.reshape(B, T, SWA_KV_HEADS, HEAD_DIM),
                          v_b.reshape(B, T, SWA_KV_HEADS, HEAD_DIM), swa_sinks[l], swa_slopes)
        merged = jax.nn.sigmoid(g_a) * (a @ moba_out_w[l]) + jax.nn.sigmoid(g_b) * (b @ swa_out_w[l])
        x = x + merged @ mix_out_w[l]
        x = x + cross_attention(rmsnorm(x, xattn_norm_g[l]), rmsnorm(mem, mem_norm_g[l]),
                                xattn_wq[l], xattn_wkv[l], xattn_wo[l])
        x = x + hier_moe(rmsnorm(x, ffn_norm_g[l]), router_group_w[l], router_group_b[l],
                         router_expert_w[l], router_expert_b[l],
                         expert_w_gate[l], expert_w_up[l], expert_w_down[l])
    return rmsnorm(x, final_norm_g)
```

```python
import functools

import jax
import jax.numpy as jnp
from jax import lax
from jax.experimental import pallas as pl
from jax.experimental.pallas import tpu as pltpu

F32 = jnp.float32
BF16 = jnp.bfloat16

D_MODEL = 1024
HEAD_DIM = 64
MOBA_HEADS = 8
MOBA_BLOCK = 256
MOBA_TOPK = 3
SWA_Q_HEADS = 8
SWA_KV_HEADS = 2
SWA_WINDOW = 128
XATTN_HEADS = 4
XATTN_HEAD_DIM = 128
N_GROUPS = 4
EXPERTS_PER_GROUP = 8
D_EXPERT = 256
RMS_EPS = 1e-6
NEG = -1e30
LANES = 128
VMEM_LIMIT = 56 * 1024 * 1024

IN_TILE = 512
MIX_TILE = 256
MOE_TILE = 512
MEM_TILE = 512


def _rms(x, g):
    return x * lax.rsqrt(jnp.mean(x * x, axis=-1, keepdims=True) + RMS_EPS) * g


def _dot(a, b):
    return jnp.dot(a, b, preferred_element_type=F32)


def _dot_nt(a, b):
    return lax.dot_general(a, b, (((1,), (1,)), ((), ())), preferred_element_type=F32)


_IN_WIDTHS = (512, 512, 512, 512, 256, 256, 1024, 1024)


def _in_proj_kernel(x_ref, g_ref, w_ref, *out_refs):
    h = _rms(x_ref[...], g_ref[...]).astype(BF16)
    c0 = 0
    for o_ref, width in zip(out_refs, _IN_WIDTHS):
        o_ref[...] = _dot(h, w_ref[:, c0:c0 + width]).astype(o_ref.dtype)
        c0 += width


def _in_proj(x2d, g, w_all):
    n = x2d.shape[0]
    tm = IN_TILE
    wtot = w_all.shape[1]
    return pl.pallas_call(
        _in_proj_kernel,
        grid=(n // tm,),
        in_specs=[
            pl.BlockSpec((tm, D_MODEL), lambda i: (i, 0)),
            pl.BlockSpec((1, D_MODEL), lambda i: (0, 0)),
            pl.BlockSpec((D_MODEL, wtot), lambda i: (0, 0)),
        ],
        out_specs=[pl.BlockSpec((tm, w), lambda i: (i, 0)) for w in _IN_WIDTHS],
        out_shape=[jax.ShapeDtypeStruct((n, w), BF16) for w in _IN_WIDTHS],
        compiler_params=pltpu.CompilerParams(
            dimension_semantics=("parallel",), vmem_limit_bytes=VMEM_LIMIT),
        name="in_proj",
    )(x2d, g, w_all)


def _moba_kernel(slopes_ref, q_ref, k_ref, v_ref, o_ref, kaug_ref):
    hp = pl.program_id(1)
    t_len = q_ref.shape[0]
    nb = t_len // MOBA_BLOCK
    bs = MOBA_BLOCK

    k = k_ref[...]
    row_k = lax.broadcasted_iota(jnp.int32, (t_len, LANES), 0)
    lane_k = lax.broadcasted_iota(jnp.int32, (t_len, LANES), 1)
    blk_k = row_k // bs
    off_k = (row_k % bs).astype(F32)
    for hh in range(2):
        slope = slopes_ref[2 * hp + hh]
        hb, fb = 64 * hh, 64 * (1 - hh)
        head_lane = (lane_k >= hb) & (lane_k < hb + 64)
        fl = lane_k - fb
        aug = jnp.where(fl == blk_k, 1.0, 0.0)
        aug = jnp.where(fl == nb, slope * off_k, aug)
        kaug_ref[hh] = jnp.where(head_lane, k, aug.astype(BF16))

    c_row = lax.broadcasted_iota(jnp.int32, (LANES, t_len), 0)
    r_blk = lax.broadcasted_iota(jnp.int32, (LANES, t_len), 1) // bs
    a1t = jnp.where((c_row < nb * nb) & ((c_row % nb) == r_blk), 1.0, 0.0).astype(BF16)
    a2t = jnp.where((c_row < nb * nb) & ((c_row // nb) == r_blk), 1.0, 0.0).astype(BF16)
    km1 = (_dot(a1t, k) * (1.0 / bs)).T.astype(BF16)
    km2 = (_dot(a2t, k) * (1.0 / bs)).T.astype(BF16)

    lane_q = lax.broadcasted_iota(jnp.int32, (bs, LANES), 1)
    pair_n = lane_q % nb
    pair_m = lane_q // nb
    rr = lax.broadcasted_iota(jnp.int32, (LANES, LANES), 0)
    ll = lax.broadcasted_iota(jnp.int32, (LANES, LANES), 1)
    s_row = lax.broadcasted_iota(jnp.int32, (bs, bs), 0)
    s_col = lax.broadcasted_iota(jnp.int32, (bs, bs), 1)
    causal = s_col <= s_row

    def q_block(i, carry):
        qs = pl.multiple_of(i * bs, bs)
        q = q_ref[pl.ds(qs, bs), :]
        v_own = v_ref[pl.ds(qs, bs), :]
        outs = []
        for hh in range(2):
            slope = slopes_ref[2 * hp + hh]
            hb, fb = 64 * hh, 64 * (1 - hh)
            head_lane = (lane_q >= hb) & (lane_q < hb + 64)
            qh = jnp.where(head_lane, q, jnp.zeros_like(q)) * jnp.asarray(HEAD_DIM ** -0.5, BF16)
            g_n = _dot(qh, km1)
            g_m = _dot(qh, km2)
            beats = ((lane_q < nb * nb) & (pair_m < i) & (pair_m != pair_n)
                     & ((g_m > g_n) | ((g_m == g_n) & (pair_m < pair_n))))
            place = jnp.where((rr < nb * nb) & ((rr % nb) == ll - fb), 1.0, 0.0).astype(BF16)
            rank = _dot(jnp.where(beats, 1.0, 0.0).astype(BF16), place)
            fl = lane_q - fb
            sel = ((rank < MOBA_TOPK - 0.5) & (fl < i)) | (fl == i)
            feat = jnp.where(sel, (slope * bs) * (fl - i).astype(F32), NEG)
            feat = jnp.where(fl == nb, 1.0, feat)
            feat = jnp.where((fl < 0) | (fl > nb), 0.0, feat)
            q_aug = jnp.where(head_lane, qh, feat.astype(BF16))

            s = _dot_nt(q_aug, kaug_ref[hh, pl.ds(qs, bs), :])
            s = jnp.where(causal, s, NEG)
            m0 = jnp.max(s, axis=1, keepdims=True)
            p = jnp.exp(s - m0)
            l0 = jnp.sum(p, axis=1, keepdims=True)
            acc0 = _dot(p.astype(BF16), v_own)

            def kv_block(j, c, hh=hh, q_aug=q_aug):
                m_i, l_i, acc = c
                ks = pl.multiple_of(j * bs, bs)
                sj = _dot_nt(q_aug, kaug_ref[hh, pl.ds(ks, bs), :])
                m_n = jnp.maximum(m_i, jnp.max(sj, axis=1, keepdims=True))
                alpha = jnp.exp(m_i - m_n)
                pj = jnp.exp(sj - m_n)
                l_n = alpha * l_i + jnp.sum(pj, axis=1, keepdims=True)
                acc_n = alpha * acc + _dot(pj.astype(BF16), v_ref[pl.ds(ks, bs), :])
                return m_n, l_n, acc_n

            _, l_f, acc_f = lax.fori_loop(0, i, kv_block, (m0, l0, acc0))
            outs.append(acc_f / l_f)
        o_ref[pl.ds(qs, bs), :] = jnp.where(lane_q < 64, outs[0], outs[1]).astype(o_ref.dtype)
        return carry

    lax.fori_loop(0, nb, q_block, 0)


def _moba(slopes, q3, k3, v3):
    b, t, w = q3.shape
    spec = pl.BlockSpec((None, t, LANES), lambda bi, hp: (bi, 0, hp))
    return pl.pallas_call(
        _moba_kernel,
        grid=(b, w // LANES),
        in_specs=[pl.BlockSpec(memory_space=pltpu.SMEM), spec, spec, spec],
        out_specs=spec,
        out_shape=jax.ShapeDtypeStruct((b, t, w), BF16),
        scratch_shapes=[pltpu.VMEM((2, t, LANES), BF16)],
        compiler_params=pltpu.CompilerParams(
            dimension_semantics=("parallel", "arbitrary"), vmem_limit_bytes=VMEM_LIMIT),
        name="moba",
    )(slopes, q3, k3, v3)


def _swa_kernel(slopes_ref, sinks_ref, q_ref, kd_ref, vd_ref, o_ref, kp_ref, vp_ref, bias_ref):
    w = SWA_WINDOW
    t_len = q_ref.shape[0]
    kp_ref[0:w, :] = jnp.zeros((w, kp_ref.shape[1]), BF16)
    vp_ref[0:w, :] = jnp.zeros((w, vp_ref.shape[1]), BF16)
    kp_ref[w:, :] = kd_ref[...]
    vp_ref[w:, :] = vd_ref[...]

    r = lax.broadcasted_iota(jnp.int32, (w, 2 * w), 0)
    c = lax.broadcasted_iota(jnp.int32, (w, 2 * w), 1)
    rel = w + r - c
    in_window = (rel >= 0) & (rel < w)
    rel_f = rel.astype(F32)
    for h in range(SWA_Q_HEADS):
        bias_ref[h] = jnp.where(in_window, -slopes_ref[h] * rel_f, NEG)
    lane_q = lax.broadcasted_iota(jnp.int32, (w, LANES), 1)

    def window(n, carry):
        qs = pl.multiple_of(n * w, w)
        not_before_start = (c >= w) | (n > 0)
        for slab in range(SWA_Q_HEADS // 2):
            kvh = slab // (SWA_Q_HEADS // SWA_KV_HEADS // 2)
            q = q_ref[pl.ds(qs, w), slab * LANES:(slab + 1) * LANES]
            kk = kp_ref[pl.ds(qs, 2 * w), kvh * LANES:(kvh + 1) * LANES]
            vv = vp_ref[pl.ds(qs, 2 * w), kvh * LANES:(kvh + 1) * LANES]
            outs = []
            for half in range(2):
                h = 2 * slab + half
                head_lane = (lane_q >= 64 * half) & (lane_q < 64 * half + 64)
                qh = jnp.where(head_lane, q, jnp.zeros_like(q)) * jnp.asarray(HEAD_DIM ** -0.5, BF16)
                s = _dot_nt(qh, kk) + bias_ref[h]
                s = jnp.where(not_before_start, s, NEG)
                sink = sinks_ref[h]
                m = jnp.maximum(jnp.max(s, axis=1, keepdims=True), sink)
                p = jnp.exp(s - m)
                l = jnp.sum(p, axis=1, keepdims=True) + jnp.exp(sink - m)
                outs.append(_dot(p.astype(BF16), vv) / l)
            o_ref[pl.ds(qs, w), slab * LANES:(slab + 1) * LANES] = (
                jnp.where(lane_q < 64, outs[0], outs[1]).astype(o_ref.dtype))
        return carry

    lax.fori_loop(0, t_len // w, window, 0)


def _swa(slopes, sinks, q3, kd3, vd3):
    b, t, wq = q3.shape
    wkv = kd3.shape[2]
    smem = pl.BlockSpec(memory_space=pltpu.SMEM)
    return pl.pallas_call(
        _swa_kernel,
        grid=(b,),
        in_specs=[smem, smem,
                  pl.BlockSpec((None, t, wq), lambda bi: (bi, 0, 0)),
                  pl.BlockSpec((None, t, wkv), lambda bi: (bi, 0, 0)),
                  pl.BlockSpec((None, t, wkv), lambda bi: (bi, 0, 0))],
        out_specs=pl.BlockSpec((None, t, wq), lambda bi: (bi, 0, 0)),
        out_shape=jax.ShapeDtypeStruct((b, t, wq), BF16),
        scratch_shapes=[pltpu.VMEM((t + SWA_WINDOW, wkv), BF16),
                        pltpu.VMEM((t + SWA_WINDOW, wkv), BF16),
                        pltpu.VMEM((SWA_Q_HEADS, SWA_WINDOW, 2 * SWA_WINDOW), F32)],
        compiler_params=pltpu.CompilerParams(
            dimension_semantics=("parallel",), vmem_limit_bytes=VMEM_LIMIT),
        name="swa",
    )(slopes, sinks, q3, kd3, vd3)


def _mem_kv_kernel(m_ref, g_ref, w_ref, o_ref):
    o_ref[...] = _dot(_rms(m_ref[...], g_ref[...]).astype(BF16), w_ref[...]).astype(o_ref.dtype)


def _mem_kv(mem2d, g, w_kv):
    n = mem2d.shape[0]
    tm = MEM_TILE
    wout = w_kv.shape[1]
    return pl.pallas_call(
        _mem_kv_kernel,
        grid=(n // tm,),
        in_specs=[pl.BlockSpec((tm, D_MODEL), lambda i: (i, 0)),
                  pl.BlockSpec((1, D_MODEL), lambda i: (0, 0)),
                  pl.BlockSpec((D_MODEL, wout), lambda i: (0, 0))],
        out_specs=pl.BlockSpec((tm, wout), lambda i: (i, 0)),
        out_shape=jax.ShapeDtypeStruct((n, wout), BF16),
        compiler_params=pltpu.CompilerParams(
            dimension_semantics=("parallel",), vmem_limit_bytes=VMEM_LIMIT),
        name="mem_kv",
    )(mem2d, g, w_kv)


_GROUP_LANE0 = N_GROUPS * EXPERTS_PER_GROUP
_LOW = -3.0e38


def _route(logits):
    lane = lax.broadcasted_iota(jnp.int32, logits.shape, 1)
    lane_f = lane.astype(F32)
    is_g = (lane >= _GROUP_LANE0) & (lane < _GROUP_LANE0 + N_GROUPS)
    gl = jnp.where(is_g, logits, _LOW)
    gmax = jnp.max(gl, axis=1, keepdims=True)
    g_p = 1.0 / jnp.sum(jnp.exp(gl - gmax), axis=1, keepdims=True)
    g_idx = jnp.min(jnp.where(gl == gmax, lane_f - _GROUP_LANE0, 1e9), axis=1, keepdims=True)
    in_group = (lane < _GROUP_LANE0) & ((lane // EXPERTS_PER_GROUP).astype(F32) == g_idx)
    el = jnp.where(in_group, logits, _LOW)
    m1 = jnp.max(el, axis=1, keepdims=True)
    i1 = jnp.min(jnp.where(el == m1, lane_f, 1e9), axis=1, keepdims=True)
    el2 = jnp.where(lane_f == i1, _LOW, el)
    m2 = jnp.max(el2, axis=1, keepdims=True)
    i2 = jnp.min(jnp.where(el2 == m2, lane_f, 1e9), axis=1, keepdims=True)
    t = jnp.exp(m2 - m1)
    w1 = g_p / (1.0 + t)
    w2 = w1 * t
    return jnp.where(lane_f == i1, w1, jnp.where(lane_f == i2, w2, 0.0))


def _mix_kernel(x_ref, a_ref, b_ref, ga_ref, gb_ref, kv_ref,
                wa_ref, wb_ref, wo_ref, g2_ref, wq_ref, wxo_ref, g3_ref,
                wrh_ref, wrl_ref, br_ref,
                x2_ref, h3_ref, comb_ref):
    pa = _dot(a_ref[...], wa_ref[...])
    pb = _dot(b_ref[...], wb_ref[...])
    merged = (jax.nn.sigmoid(ga_ref[...].astype(F32)) * pa
              + jax.nn.sigmoid(gb_ref[...].astype(F32)) * pb)
    x1 = x_ref[...] + _dot(merged.astype(BF16), wo_ref[...])

    q = _dot(_rms(x1, g2_ref[...]).astype(BF16), wq_ref[...]).astype(BF16)
    width = XATTN_HEADS * XATTN_HEAD_DIM
    heads = []
    for hd in range(XATTN_HEADS):
        lo = hd * XATTN_HEAD_DIM
        kx = kv_ref[:, lo:lo + XATTN_HEAD_DIM]
        vx = kv_ref[:, width + lo:width + lo + XATTN_HEAD_DIM]
        s = _dot_nt(q[:, lo:lo + XATTN_HEAD_DIM], kx) * (XATTN_HEAD_DIM ** -0.5)
        m = jnp.max(s, axis=1, keepdims=True)
        p = jnp.exp(s - m)
        l = jnp.sum(p, axis=1, keepdims=True)
        heads.append((_dot(p.astype(BF16), vx) / l).astype(BF16))
    x2 = x1 + _dot(jnp.concatenate(heads, axis=1), wxo_ref[...])
    x2_ref[...] = x2

    h3 = _rms(x2, g3_ref[...])
    h3_hi = h3.astype(BF16)
    h3_lo = (h3 - h3_hi.astype(F32)).astype(BF16)
    h3_ref[...] = h3_hi
    logits = (_dot(h3_hi, wrh_ref[...])
              + (_dot(h3_hi, wrl_ref[...]) + _dot(h3_lo, wrh_ref[...]))
              + br_ref[...])
    comb_ref[...] = _route(logits)


def _mix(x2d, a, b, ga, gb, kv3, wa, wb, wo, g2, wq, wxo, g3, wrh, wrl, br, seq_len):
    n = x2d.shape[0]
    tm = MIX_TILE
    tiles_per_seq = seq_len // tm
    row = lambda w: pl.BlockSpec((tm, w), lambda i: (i, 0))
    full = lambda arr: pl.BlockSpec(arr.shape, lambda i: (0,) * arr.ndim)
    mem_len, kvw = kv3.shape[1], kv3.shape[2]
    return pl.pallas_call(
        _mix_kernel,
        grid=(n // tm,),
        in_specs=[row(D_MODEL), row(a.shape[1]), row(b.shape[1]), row(D_MODEL), row(D_MODEL),
                  pl.BlockSpec((None, mem_len, kvw), lambda i: (i // tiles_per_seq, 0, 0)),
                  full(wa), full(wb), full(wo), full(g2), full(wq), full(wxo), full(g3),
                  full(wrh), full(wrl), full(br)],
        out_specs=[row(D_MODEL), row(D_MODEL), row(LANES)],
        out_shape=[jax.ShapeDtypeStruct((n, D_MODEL), F32),
                   jax.ShapeDtypeStruct((n, D_MODEL), BF16),
                   jax.ShapeDtypeStruct((n, LANES), F32)],
        compiler_params=pltpu.CompilerParams(
            dimension_semantics=("parallel",), vmem_limit_bytes=VMEM_LIMIT),
        name="mix",
    )(x2d, a, b, ga, gb, kv3, wa, wb, wo, g2, wq, wxo, g3, wrh, wrl, br)


def _moe_kernel(h_ref, comb_ref, x_ref, wg_ref, wu_ref, wd_ref, gf_ref, o_ref, acc_ref, hid_ref):
    g = pl.program_id(1)
    h = h_ref[...]
    comb = comb_ref[...]
    lane = lax.broadcasted_iota(jnp.int32, comb.shape, 1)
    for e in range(EXPERTS_PER_GROUP):
        gate = _dot(h, wg_ref[e])
        up = _dot(h, wu_ref[e])
        c = jnp.sum(jnp.where(lane == g * EXPERTS_PER_GROUP + e, comb, 0.0), axis=1, keepdims=True)
        hid_ref[:, e * D_EXPERT:(e + 1) * D_EXPERT] = (
            gate * jax.nn.sigmoid(gate) * up * c).astype(BF16)
    y = _dot(hid_ref[...], wd_ref[...])

    @pl.when(g == 0)
    def _():
        acc_ref[...] = y

    @pl.when(g > 0)
    def _():
        acc_ref[...] += y

    @pl.when(g == N_GROUPS - 1)
    def _():
        o_ref[...] = _rms(x_ref[...] + acc_ref[...], gf_ref[...])


def _moe(h3, comb, x2, wg, wu, wd, gf):
    n = h3.shape[0]
    tm = MOE_TILE
    return pl.pallas_call(
        _moe_kernel,
        grid=(n // tm, N_GROUPS),
        in_specs=[pl.BlockSpec((tm, D_MODEL), lambda i, g: (i, 0)),
                  pl.BlockSpec((tm, LANES), lambda i, g: (i, 0)),
                  pl.BlockSpec((tm, D_MODEL), lambda i, g: (i, 0)),
                  pl.BlockSpec((None, EXPERTS_PER_GROUP, D_MODEL, D_EXPERT), lambda i, g: (g, 0, 0, 0)),
                  pl.BlockSpec((None, EXPERTS_PER_GROUP, D_MODEL, D_EXPERT), lambda i, g: (g, 0, 0, 0)),
                  pl.BlockSpec((None, EXPERTS_PER_GROUP * D_EXPERT, D_MODEL), lambda i, g: (g, 0, 0)),
                  pl.BlockSpec((1, D_MODEL), lambda i, g: (0, 0))],
        out_specs=pl.BlockSpec((tm, D_MODEL), lambda i, g: (i, 0)),
        out_shape=jax.ShapeDtypeStruct((n, D_MODEL), F32),
        scratch_shapes=[pltpu.VMEM((tm, D_MODEL), F32),
                        pltpu.VMEM((tm, EXPERTS_PER_GROUP * D_EXPERT), BF16)],
        compiler_params=pltpu.CompilerParams(
            dimension_semantics=("parallel", "arbitrary"), vmem_limit_bytes=VMEM_LIMIT),
        name="moe",
    )(h3, comb, x2, wg, wu, wd, gf)


def _alibi_slopes(n_heads):
    return jnp.asarray([2.0 ** (-8.0 * (i + 1) / n_heads) for i in range(n_heads)], dtype=F32)


def _split_bf16(w):
    hi = w.astype(BF16)
    return hi, (w - hi.astype(F32)).astype(BF16)


def kernel(x, mem, mix_norm_g, w_in, moba_out_w, swa_out_w, swa_sinks, mix_out_w, xattn_norm_g, mem_norm_g, xattn_wq, xattn_wkv, xattn_wo, ffn_norm_g, router_group_w, router_group_b, router_expert_w, router_expert_b, expert_w_gate, expert_w_up, expert_w_down, final_norm_g):
    bsz, t, d = x.shape
    n = bsz * t
    assert w_in.shape[0] == 1, "the final RMSNorm is fused into the (single) layer's MoE kernel"
    x2d = x.reshape(n, d)
    mem2d = mem.reshape(-1, d)
    mem_len = mem.shape[1]
    moba_slopes = _alibi_slopes(MOBA_HEADS)
    swa_slopes = _alibi_slopes(SWA_Q_HEADS)
    out = None
    for l in range(1):
        w = w_in[l]
        c = 3 * MOBA_HEADS * HEAD_DIM + SWA_Q_HEADS * HEAD_DIM
        kvw = SWA_KV_HEADS * HEAD_DIM
        dup = lambda m: jnp.concatenate(
            [m[:, j * HEAD_DIM:(j + 1) * HEAD_DIM] for j in range(SWA_KV_HEADS) for _ in range(2)], axis=1)
        w_all = jnp.concatenate(
            [w[:, :c], dup(w[:, c:c + kvw]), dup(w[:, c + kvw:c + 2 * kvw]), w[:, c + 2 * kvw:]],
            axis=1).astype(BF16)
        qa, ka, va, qb, kd, vd, ga, gb = _in_proj(x2d, mix_norm_g[l].reshape(1, d), w_all)

        r3 = lambda arr: arr.reshape(bsz, t, arr.shape[1])
        a = _moba(moba_slopes, r3(qa), r3(ka), r3(va)).reshape(n, -1)
        b = _swa(swa_slopes, swa_sinks[l], r3(qb), r3(kd), r3(vd)).reshape(n, -1)

        kv = _mem_kv(mem2d, mem_norm_g[l].reshape(1, d), xattn_wkv[l].astype(BF16))
        kv3 = kv.reshape(bsz, mem_len, kv.shape[1])

        pad = LANES - _GROUP_LANE0 - N_GROUPS
        w_r = jnp.concatenate([router_expert_w[l], router_group_w[l], jnp.zeros((d, pad), F32)], axis=1)
        b_r = jnp.concatenate([router_expert_b[l], router_group_b[l], jnp.zeros((pad,), F32)]).reshape(1, LANES)
        wrh, wrl = _split_bf16(w_r)
        x2d, h3, comb = _mix(
            x2d, a, b, ga, gb, kv3,
            moba_out_w[l].astype(BF16), swa_out_w[l].astype(BF16), mix_out_w[l].astype(BF16),
            xattn_norm_g[l].reshape(1, d), xattn_wq[l].astype(BF16), xattn_wo[l].astype(BF16),
            ffn_norm_g[l].reshape(1, d), wrh, wrl, b_r, t)

        gf = final_norm_g.reshape(1, d)
        wd =expert_w_down[l].astype(BF16).reshape(N_GROUPS, EXPERTS_PER_GROUP * D_EXPERT, d)
        out = _moe(h3, comb, x2d, expert_w_gate[l].astype(BF16), expert_w_up[l].astype(BF16), wd, gf)
        x2d = out
    return out.reshape(bsz, t, d)
```

```python
import numpy as np

import jax
import jax.numpy as jnp
from jax import lax
from jax.experimental import pallas as pl
from jax.experimental.pallas import tpu as pltpu

F32 = jnp.float32
BF16 = jnp.bfloat16

D_MODEL = 1024
HEAD_DIM = 64
MOBA_HEADS = 8
MOBA_BLOCK = 256
MOBA_TOPK = 3
SWA_Q_HEADS = 8
SWA_KV_HEADS = 2
SWA_WINDOW = 128
XATTN_HEADS = 4
XATTN_HEAD_DIM = 128
N_GROUPS = 4
EXPERTS_PER_GROUP = 8
D_EXPERT = 256
RMS_EPS = 1e-6
NEG = -1e30
LANES = 128
VMEM_LIMIT = 56 * 1024 * 1024

IN_TILE = 512
MIX_TILE = 256
MOE_TILE = 512
MEM_TILE = 512


def _rms(x, g):
    return x * lax.rsqrt(jnp.mean(x * x, axis=-1, keepdims=True) + RMS_EPS) * g


def _dot(a, b):
    return jnp.dot(a, b, preferred_element_type=F32)


def _dot_nt(a, b):
    return lax.dot_general(a, b, (((1,), (1,)), ((), ())), preferred_element_type=F32)


_IN_WIDTHS = (512, 512, 512, 512, 256, 256, 1024, 1024)


def _in_proj_kernel(x_ref, g_ref, w_ref, *out_refs):
    h = _rms(x_ref[...], g_ref[...]).astype(BF16)
    c0 = 0
    for o_ref, width in zip(out_refs, _IN_WIDTHS):
        o_ref[...] = _dot(h, w_ref[:, c0:c0 + width]).astype(o_ref.dtype)
        c0 += width


def _in_proj(x2d, g, w_all):
    n = x2d.shape[0]
    tm = IN_TILE
    wtot = w_all.shape[1]
    return pl.pallas_call(
        _in_proj_kernel,
        grid=(n // tm,),
        in_specs=[
            pl.BlockSpec((tm, D_MODEL), lambda i: (i, 0)),
            pl.BlockSpec((1, D_MODEL), lambda i: (0, 0)),
            pl.BlockSpec((D_MODEL, wtot), lambda i: (0, 0)),
        ],
        out_specs=[pl.BlockSpec((tm, w), lambda i: (i, 0)) for w in _IN_WIDTHS],
        out_shape=[jax.ShapeDtypeStruct((n, w), BF16) for w in _IN_WIDTHS],
        compiler_params=pltpu.CompilerParams(
            dimension_semantics=("parallel",), vmem_limit_bytes=VMEM_LIMIT),
        name="in_proj",
    )(x2d, g, w_all)


_OFF_LANE = 8


def _moba_constants(t_len):
    nb = t_len // MOBA_BLOCK
    r = np.arange(t_len)
    kfeat = np.zeros((t_len, LANES), np.float32)
    kfeat[r, r // MOBA_BLOCK] = 1.0
    kfeat[:, _OFF_LANE] = r % MOBA_BLOCK
    c = np.arange(nb * nb)
    pairs = np.zeros((2 * LANES, t_len), np.float32)
    pairs[c[:, None], r[None, :]] = ((c % nb)[:, None] == (r // MOBA_BLOCK)[None, :])
    pairs[LANES + c[:, None], r[None, :]] = ((c // nb)[:, None] == (r // MOBA_BLOCK)[None, :])
    place = np.zeros((nb, LANES, LANES), np.float32)
    for i in range(nb):
        for m in range(i):
            for n in range(nb):
                if m != n:
                    place[i, nb * m + n, n] = 1.0
    return (jnp.asarray(kfeat, BF16), jnp.asarray(pairs, BF16), jnp.asarray(place, BF16))


def _moba_kernel(slopes_ref, q_ref, k_ref, v_ref, kfeat_ref, pairs_ref, place_ref, o_ref, kaug_ref):
    hp = pl.program_id(1)
    t_len = q_ref.shape[0]
    nb = t_len // MOBA_BLOCK
    bs = MOBA_BLOCK

    k = k_ref[...]
    kaug_ref[:, 0:LANES] = k
    kaug_ref[:, LANES:2 * LANES] = kfeat_ref[...]
    kmean_t = _dot(pairs_ref[...], k) * (1.0 / bs)
    km = jnp.concatenate([kmean_t[0:LANES].T, kmean_t[LANES:2 * LANES].T], axis=1).astype(BF16)

    lane = lax.broadcasted_iota(jnp.int32, (bs, LANES), 1)
    lane2 = lax.broadcasted_iota(jnp.int32, (2 * bs, LANES), 1)
    rival_first = (lane2 // nb) < (lane2 % nb)
    s_row = lax.broadcasted_iota(jnp.int32, (2 * bs, bs), 0)
    s_col = lax.broadcasted_iota(jnp.int32, (2 * bs, bs), 1)
    causal = s_col <= (s_row % bs)
    slopes = (slopes_ref[2 * hp], slopes_ref[2 * hp + 1])
    scale = jnp.asarray(HEAD_DIM ** -0.5, BF16)

    for i in range(nb):
        q = q_ref[i * bs:(i + 1) * bs, :]
        zero = jnp.zeros_like(q)
        qs = jnp.concatenate([jnp.where(lane < 64, q, zero) * scale,
                              jnp.where(lane >= 64, q, zero) * scale], axis=0)
        gates = _dot(qs, km)
        g_n, g_m = gates[:, 0:LANES], gates[:, LANES:2 * LANES]
        beats = (g_m > g_n) | (rival_first & (g_m == g_n))
        rank = _dot(jnp.where(beats, 1.0, 0.0).astype(BF16), place_ref[i])
        chosen = (rank < MOBA_TOPK - 0.5) & (lane2 < i)
        feats = []
        for hh in range(2):
            picked = (slopes[hh] * bs) * (lane - i).astype(F32)
            other = jnp.where(lane == _OFF_LANE, slopes[hh], jnp.where(lane < i, NEG, 0.0))
            feats.append(jnp.where(chosen[hh * bs:(hh + 1) * bs], picked, other))
        q_aug = jnp.concatenate([qs, jnp.concatenate(feats, axis=0).astype(BF16)], axis=1)

        m_i = l_i = acc = None
        for j in [i] + list(range(i)):
            s = _dot_nt(q_aug, kaug_ref[j * bs:(j + 1) * bs, :])
            v_j = v_ref[j * bs:(j + 1) * bs, :]
            if j == i:
                s = jnp.where(causal, s, NEG)
                m_i = jnp.max(s, axis=1, keepdims=True)
                p = jnp.exp(s - m_i)
                l_i = jnp.sum(p, axis=1, keepdims=True)
                acc = _dot(p.astype(BF16), v_j)
            else:
                m_n = jnp.maximum(m_i, jnp.max(s, axis=1, keepdims=True))
                alpha = jnp.exp(m_i - m_n)
                p = jnp.exp(s - m_n)
                l_i = alpha * l_i + jnp.sum(p, axis=1, keepdims=True)
                acc = alpha * acc + _dot(p.astype(BF16), v_j)
                m_i = m_n
        o = acc / l_i
        o_ref[i * bs:(i + 1) * bs, :] = jnp.where(lane < 64, o[0:bs], o[bs:2 * bs]).astype(o_ref.dtype)


def _moba(slopes, q3, k3, v3):
    b, t, w = q3.shape
    kfeat, pairs, place = _moba_constants(t)
    spec = pl.BlockSpec((None, t, LANES), lambda bi, hp: (bi, 0, hp))
    const = lambda arr: pl.BlockSpec(arr.shape, lambda bi, hp: (0,) * arr.ndim)
    return pl.pallas_call(
        _moba_kernel,
        grid=(b, w // LANES),
        in_specs=[pl.BlockSpec(memory_space=pltpu.SMEM), spec, spec, spec,
                  const(kfeat), const(pairs), const(place)],
        out_specs=spec,
        out_shape=jax.ShapeDtypeStruct((b, t, w), BF16),
        scratch_shapes=[pltpu.VMEM((t, 2 * LANES), BF16)],
        compiler_params=pltpu.CompilerParams(
            dimension_semantics=("parallel", "arbitrary"), vmem_limit_bytes=VMEM_LIMIT),
        name="moba",
    )(slopes, q3, k3, v3, kfeat, pairs, place)


def _swa_kernel(slopes_ref, sinks_ref, q_ref, kd_ref, vd_ref, o_ref, kp_ref, vp_ref, bias_ref):
    w = SWA_WINDOW
    t_len = q_ref.shape[0]
    kp_ref[0:w, :] = jnp.zeros((w, kp_ref.shape[1]), BF16)
    vp_ref[0:w, :] = jnp.zeros((w, vp_ref.shape[1]), BF16)
    kp_ref[w:, :] = kd_ref[...]
    vp_ref[w:, :] = vd_ref[...]

    r = lax.broadcasted_iota(jnp.int32, (w, 2 * w), 0)
    c = lax.broadcasted_iota(jnp.int32, (w, 2 * w), 1)
    rel = w + r - c
    in_window = (rel >= 0) & (rel < w)
    rel_f = rel.astype(F32)
    for h in range(SWA_Q_HEADS):
        bias_ref[h] = jnp.where(in_window, -slopes_ref[h] * rel_f, NEG)
    lane_q = lax.broadcasted_iota(jnp.int32, (w, LANES), 1)

    def window(n, carry):
        qs = pl.multiple_of(n * w, w)
        not_before_start = (c >= w) | (n > 0)
        for slab in range(SWA_Q_HEADS // 2):
            kvh = slab // (SWA_Q_HEADS // SWA_KV_HEADS // 2)
            q = q_ref[pl.ds(qs, w), slab * LANES:(slab + 1) * LANES]
            kk = kp_ref[pl.ds(qs, 2 * w), kvh * LANES:(kvh + 1) * LANES]
            vv = vp_ref[pl.ds(qs, 2 * w), kvh * LANES:(kvh + 1) * LANES]
            outs = []
            for half in range(2):
                h = 2 * slab + half
                head_lane = (lane_q >= 64 * half) & (lane_q < 64 * half + 64)
                qh = jnp.where(head_lane, q, jnp.zeros_like(q)) * jnp.asarray(HEAD_DIM ** -0.5, BF16)
                s = _dot_nt(qh, kk) + bias_ref[h]
                s = jnp.where(not_before_start, s, NEG)
                sink = sinks_ref[h]
                m = jnp.maximum(jnp.max(s, axis=1, keepdims=True), sink)
                p = jnp.exp(s - m)
                l = jnp.sum(p, axis=1, keepdims=True) + jnp.exp(sink - m)
                outs.append(_dot(p.astype(BF16), vv) / l)
            o_ref[pl.ds(qs, w), slab * LANES:(slab + 1) * LANES] = (
                jnp.where(lane_q < 64, outs[0], outs[1]).astype(o_ref.dtype))
        return carry

    lax.fori_loop(0, t_len // w, window, 0)


def _swa(slopes, sinks, q3, kd3, vd3):
    b, t, wq = q3.shape
    wkv = kd3.shape[2]
    smem = pl.BlockSpec(memory_space=pltpu.SMEM)
    return pl.pallas_call(
        _swa_kernel,
        grid=(b,),
        in_specs=[smem, smem,
                  pl.BlockSpec((None, t, wq), lambda bi: (bi, 0, 0)),
                  pl.BlockSpec((None, t, wkv), lambda bi: (bi, 0, 0)),
                  pl.BlockSpec((None, t, wkv), lambda bi: (bi, 0, 0))],
        out_specs=pl.BlockSpec((None, t, wq), lambda bi: (bi, 0, 0)),
        out_shape=jax.ShapeDtypeStruct((b, t, wq), BF16),
        scratch_shapes=[pltpu.VMEM((t + SWA_WINDOW, wkv), BF16),
                        pltpu.VMEM((t + SWA_WINDOW, wkv), BF16),
                        pltpu.VMEM((SWA_Q_HEADS, SWA_WINDOW, 2 * SWA_WINDOW), F32)],
        compiler_params=pltpu.CompilerParams(
            dimension_semantics=("parallel",), vmem_limit_bytes=VMEM_LIMIT),
        name="swa",
    )(slopes, sinks, q3, kd3, vd3)


def _mem_kv_kernel(m_ref, g_ref, w_ref, o_ref):
    o_ref[...] = _dot(_rms(m_ref[...], g_ref[...]).astype(BF16), w_ref[...]).astype(o_ref.dtype)


def _mem_kv(mem2d, g, w_kv):
    n = mem2d.shape[0]
    tm = MEM_TILE
    wout = w_kv.shape[1]
    return pl.pallas_call(
        _mem_kv_kernel,
        grid=(n // tm,),
        in_specs=[pl.BlockSpec((tm, D_MODEL), lambda i: (i, 0)),
                  pl.BlockSpec((1, D_MODEL), lambda i: (0, 0)),
                  pl.BlockSpec((D_MODEL, wout), lambda i: (0, 0))],
        out_specs=pl.BlockSpec((tm, wout), lambda i: (i, 0)),
        out_shape=jax.ShapeDtypeStruct((n, wout), BF16),
        compiler_params=pltpu.CompilerParams(
            dimension_semantics=("parallel",), vmem_limit_bytes=VMEM_LIMIT),
        name="mem_kv",
    )(mem2d, g, w_kv)


_GROUP_LANE0 = N_GROUPS * EXPERTS_PER_GROUP
_LOW = -3.0e38


def _route(logits):
    lane = lax.broadcasted_iota(jnp.int32, logits.shape, 1)
    lane_f = lane.astype(F32)
    is_g = (lane >= _GROUP_LANE0) & (lane < _GROUP_LANE0 + N_GROUPS)
    gl = jnp.where(is_g, logits, _LOW)
    gmax = jnp.max(gl, axis=1, keepdims=True)
    g_p = 1.0 / jnp.sum(jnp.exp(gl - gmax), axis=1, keepdims=True)
    g_idx = jnp.min(jnp.where(gl == gmax, lane_f - _GROUP_LANE0, 1e9), axis=1, keepdims=True)
    in_group = (lane < _GROUP_LANE0) & ((lane // EXPERTS_PER_GROUP).astype(F32) == g_idx)
    el = jnp.where(in_group, logits, _LOW)
    m1 = jnp.max(el, axis=1, keepdims=True)
    i1 = jnp.min(jnp.where(el == m1, lane_f, 1e9), axis=1, keepdims=True)
    el2 = jnp.where(lane_f == i1, _LOW, el)
    m2 = jnp.max(el2, axis=1, keepdims=True)
    i2 = jnp.min(jnp.where(el2 == m2, lane_f, 1e9), axis=1, keepdims=True)
    t = jnp.exp(m2 - m1)
    w1 = g_p / (1.0 + t)
    w2 = w1 * t
    return jnp.where(lane_f == i1, w1, jnp.where(lane_f == i2, w2, 0.0))


def _mix_kernel(x_ref, a_ref, b_ref, ga_ref, gb_ref, kv_ref,
                wa_ref, wb_ref, wo_ref, g2_ref, wq_ref, wxo_ref, g3_ref,
                wrh_ref, wrl_ref, br_ref,
                x2_ref, h3_ref, comb_ref):
    pa = _dot(a_ref[...], wa_ref[...])
    pb = _dot(b_ref[...], wb_ref[...])
    merged = (jax.nn.sigmoid(ga_ref[...].astype(F32)) * pa
              + jax.nn.sigmoid(gb_ref[...].astype(F32)) * pb)
    x1 = x_ref[...] + _dot(merged.astype(BF16), wo_ref[...])

    q = _dot(_rms(x1, g2_ref[...]).astype(BF16), wq_ref[...]).astype(BF16)
    width = XATTN_HEADS * XATTN_HEAD_DIM
    heads = []
    for hd in range(XATTN_HEADS):
        lo = hd * XATTN_HEAD_DIM
        kx = kv_ref[:, lo:lo + XATTN_HEAD_DIM]
        vx = kv_ref[:, width + lo:width + lo + XATTN_HEAD_DIM]
        s = _dot_nt(q[:, lo:lo + XATTN_HEAD_DIM], kx) * (XATTN_HEAD_DIM ** -0.5)
        m = jnp.max(s, axis=1, keepdims=True)
        p = jnp.exp(s - m)
        l = jnp.sum(p, axis=1, keepdims=True)
        heads.append((_dot(p.astype(BF16), vx) / l).astype(BF16))
    x2 = x1 + _dot(jnp.concatenate(heads, axis=1), wxo_ref[...])
    x2_ref[...] = x2

    h3 = _rms(x2, g3_ref[...])
    h3_hi = h3.astype(BF16)
    h3_lo = (h3 - h3_hi.astype(F32)).astype(BF16)
    h3_ref[...] = h3_hi
    logits = (_dot(h3_hi, wrh_ref[...])
              + (_dot(h3_hi, wrl_ref[...]) + _dot(h3_lo, wrh_ref[...]))
              + br_ref[...])
    comb_ref[...] = _route(logits)


def _mix(x2d, a, b, ga, gb, kv3, wa, wb, wo, g2, wq, wxo, g3, wrh, wrl, br, seq_len):
    n = x2d.shape[0]
    tm = MIX_TILE
    tiles_per_seq = seq_len // tm
    row = lambda w: pl.BlockSpec((tm, w), lambda i: (i, 0))
    full = lambda arr: pl.BlockSpec(arr.shape, lambda i: (0,) * arr.ndim)
    mem_len, kvw = kv3.shape[1], kv3.shape[2]
    return pl.pallas_call(
        _mix_kernel,
        grid=(n // tm,),
        in_specs=[row(D_MODEL), row(a.shape[1]), row(b.shape[1]), row(D_MODEL), row(D_MODEL),
                  pl.BlockSpec((None, mem_len, kvw), lambda i: (i // tiles_per_seq, 0, 0)),
                  full(wa), full(wb), full(wo), full(g2), full(wq), full(wxo), full(g3),
                  full(wrh), full(wrl), full(br)],
        out_specs=[row(D_MODEL), row(D_MODEL), row(LANES)],
        out_shape=[jax.ShapeDtypeStruct((n, D_MODEL), F32),
                   jax.ShapeDtypeStruct((n, D_MODEL), BF16),
                   jax.ShapeDtypeStruct((n, LANES), F32)],
        compiler_params=pltpu.CompilerParams(
            dimension_semantics=("parallel",), vmem_limit_bytes=VMEM_LIMIT),
        name="mix",
    )(x2d, a, b, ga, gb, kv3, wa, wb, wo, g2, wq, wxo, g3, wrh, wrl, br)


def _moe_kernel(h_ref, comb_ref, x_ref, wg_ref, wu_ref, wd_ref, gf_ref, o_ref, acc_ref, hid_ref):
    g = pl.program_id(1)
    h = h_ref[...]
    comb = comb_ref[...]
    lane = lax.broadcasted_iota(jnp.int32, comb.shape, 1)
    for e in range(EXPERTS_PER_GROUP):
        gate = _dot(h, wg_ref[e])
        up = _dot(h, wu_ref[e])
        c = jnp.sum(jnp.where(lane == g * EXPERTS_PER_GROUP + e, comb, 0.0), axis=1, keepdims=True)
        hid_ref[:, e * D_EXPERT:(e + 1) * D_EXPERT] = (
            gate * jax.nn.sigmoid(gate) * up * c).astype(BF16)
    y = _dot(hid_ref[...], wd_ref[...])

    @pl.when(g == 0)
    def _():
        acc_ref[...] = y

    @pl.when(g > 0)
    def _():
        acc_ref[...] += y

    @pl.when(g == N_GROUPS - 1)
    def _():
        o_ref[...] = _rms(x_ref[...] + acc_ref[...], gf_ref[...])


def _moe(h3, comb, x2, wg, wu, wd, gf):
    n = h3.shape[0]
    tm = MOE_TILE
    return pl.pallas_call(
        _moe_kernel,
        grid=(n // tm, N_GROUPS),
        in_specs=[pl.BlockSpec((tm, D_MODEL), lambda i, g: (i, 0)),
                  pl.BlockSpec((tm, LANES), lambda i, g: (i, 0)),
                  pl.BlockSpec((tm, D_MODEL), lambda i, g: (i, 0)),
                  pl.BlockSpec((None, EXPERTS_PER_GROUP, D_MODEL, D_EXPERT), lambda i, g: (g, 0, 0, 0)),
                  pl.BlockSpec((None, EXPERTS_PER_GROUP, D_MODEL, D_EXPERT), lambda i, g: (g, 0, 0, 0)),
                  pl.BlockSpec((None, EXPERTS_PER_GROUP * D_EXPERT, D_MODEL), lambda i, g: (g, 0, 0)),
                  pl.BlockSpec((1, D_MODEL), lambda i, g: (0, 0))],
        out_specs=pl.BlockSpec((tm, D_MODEL), lambda i, g: (i, 0)),
        out_shape=jax.ShapeDtypeStruct((n, D_MODEL), F32),
        scratch_shapes=[pltpu.VMEM((tm, D_MODEL), F32),
                        pltpu.VMEM((tm, EXPERTS_PER_GROUP * D_EXPERT), BF16)],
        compiler_params=pltpu.CompilerParams(
            dimension_semantics=("parallel", "arbitrary"), vmem_limit_bytes=VMEM_LIMIT),
        name="moe",
    )(h3, comb, x2, wg, wu, wd, gf)


def _alibi_slopes(n_heads):
    return jnp.asarray([2.0 ** (-8.0 * (i + 1) / n_heads) for i in range(n_heads)], dtype=F32)


def _split_bf16(w):
    hi = w.astype(BF16)
    return hi, (w - hi.astype(F32)).astype(BF16)


def kernel(x, mem, mix_norm_g, w_in, moba_out_w, swa_out_w, swa_sinks, mix_out_w, xattn_norm_g, mem_norm_g, xattn_wq, xattn_wkv, xattn_wo, ffn_norm_g, router_group_w, router_group_b, router_expert_w, router_expert_b, expert_w_gate, expert_w_up, expert_w_down, final_norm_g):
    bsz, t, d = x.shape
    n = bsz * t
    assert w_in.shape[0] == 1, "the final RMSNorm is fused into the (single) layer's MoE kernel"
    l = 0
    x2d = x.reshape(n, d)
    mem2d = mem.reshape(-1, d)
    mem_len = mem.shape[1]
    moba_slopes = _alibi_slopes(MOBA_HEADS)
    swa_slopes = _alibi_slopes(SWA_Q_HEADS)

    w = w_in[l]
    c = 3 * MOBA_HEADS * HEAD_DIM + SWA_Q_HEADS * HEAD_DIM
    kvw = SWA_KV_HEADS * HEAD_DIM
    dup = lambda m: jnp.concatenate(
        [m[:, j * HEAD_DIM:(j + 1) * HEAD_DIM] for j in range(SWA_KV_HEADS) for _ in range(2)], axis=1)
    w_all = jnp.concatenate(
        [w[:, :c], dup(w[:, c:c + kvw]), dup(w[:, c + kvw:c + 2 * kvw]), w[:, c + 2 * kvw:]],
        axis=1).astype(BF16)
    qa, ka, va, qb, kd, vd, ga, gb = _in_proj(x2d, mix_norm_g[l].reshape(1, d), w_all)

    r3 = lambda arr: arr.reshape(bsz, t, arr.shape[1])
    a = _moba(moba_slopes, r3(qa), r3(ka), r3(va)).reshape(n, -1)
    b = _swa(swa_slopes, swa_sinks[l], r3(qb), r3(kd), r3(vd)).reshape(n, -1)

    kv = _mem_kv(mem2d, mem_norm_g[l].reshape(1, d), xattn_wkv[l].astype(BF16))
    kv3 = kv.reshape(bsz, mem_len, kv.shape[1])

    pad = LANES - _GROUP_LANE0 - N_GROUPS
    w_r = jnp.concatenate([router_expert_w[l], router_group_w[l], jnp.zeros((d, pad), F32)], axis=1)
    b_r = jnp.concatenate([router_expert_b[l], router_group_b[l], jnp.zeros((pad,), F32)]).reshape(1, LANES)
    wrh, wrl = _split_bf16(w_r)
    x2, h3, comb = _mix(
        x2d, a, b, ga, gb, kv3,
        moba_out_w[l].astype(BF16), swa_out_w[l].astype(BF16), mix_out_w[l].astype(BF16),
        xattn_norm_g[l].reshape(1, d), xattn_wq[l].astype(BF16), xattn_wo[l].astype(BF16),
        ffn_norm_g[l].reshape(1, d), wrh, wrl, b_r, t)

    wd = expert_w_down[l].astype(BF16).reshape(N_GROUPS, EXPERTS_PER_GROUP * D_EXPERT, d)
    out = _moe(h3, comb, x2, expert_w_gate[l].astype(BF16), expert_w_up[l].astype(BF16), wd,
               final_norm_g.reshape(1, d))
    return out.reshape(bsz, t, d)
```

```python
import numpy as np

import jax
import jax.numpy as jnp
from jax import lax
from jax.experimental import pallas as pl
from jax.experimental.pallas import tpu as pltpu

F32 = jnp.float32
BF16 = jnp.bfloat16

D_MODEL = 1024
HEAD_DIM = 64
MOBA_HEADS = 8
MOBA_BLOCK = 256
MOBA_TOPK = 3
SWA_Q_HEADS = 8
SWA_KV_HEADS = 2
SWA_WINDOW = 128
XATTN_HEADS = 4
XATTN_HEAD_DIM = 128
N_GROUPS = 4
EXPERTS_PER_GROUP = 8
D_EXPERT = 256
RMS_EPS = 1e-6
NEG = -1e30
LANES = 128
VMEM_LIMIT = 56 * 1024 * 1024

IN_TILE = 512
MIX_TILE = 256
MOE_TILE = 1024
MOE_CHUNK = LANES
MEM_TILE = 512


def _rms(x, g):
    return x * lax.rsqrt(jnp.mean(x * x, axis=-1, keepdims=True) + RMS_EPS) * g


def _dot(a, b):
    return jnp.dot(a, b, preferred_element_type=F32)


def _dot_nt(a, b):
    return lax.dot_general(a, b, (((1,), (1,)), ((), ())), preferred_element_type=F32)


_IN_WIDTHS = (512, 512, 512, 512, 256, 256, 1024, 1024)


def _in_proj_kernel(x_ref, g_ref, w_ref, *out_refs):
    h = _rms(x_ref[...], g_ref[...]).astype(BF16)
    c0 = 0
    for o_ref, width in zip(out_refs, _IN_WIDTHS):
        o_ref[...] = _dot(h, w_ref[:, c0:c0 + width]).astype(o_ref.dtype)
        c0 += width


def _in_proj(x2d, g, w_all):
    n = x2d.shape[0]
    tm = IN_TILE
    wtot = w_all.shape[1]
    return pl.pallas_call(
        _in_proj_kernel,
        grid=(n // tm,),
        in_specs=[
            pl.BlockSpec((tm, D_MODEL), lambda i: (i, 0)),
            pl.BlockSpec((1, D_MODEL), lambda i: (0, 0)),
            pl.BlockSpec((D_MODEL, wtot), lambda i: (0, 0)),
        ],
        out_specs=[pl.BlockSpec((tm, w), lambda i: (i, 0)) for w in _IN_WIDTHS],
        out_shape=[jax.ShapeDtypeStruct((n, w), BF16) for w in _IN_WIDTHS],
        compiler_params=pltpu.CompilerParams(
            dimension_semantics=("parallel",), vmem_limit_bytes=VMEM_LIMIT),
        name="in_proj",
    )(x2d, g, w_all)


_OFF_LANE = 8


def _moba_constants(t_len):
    nb = t_len // MOBA_BLOCK
    r = np.arange(t_len)
    kfeat = np.zeros((t_len, LANES), np.float32)
    kfeat[r, r // MOBA_BLOCK] = 1.0
    kfeat[:, _OFF_LANE] = r % MOBA_BLOCK
    c = np.arange(nb * nb)
    pairs = np.zeros((2 * LANES, t_len), np.float32)
    pairs[c[:, None], r[None, :]] = ((c % nb)[:, None] == (r // MOBA_BLOCK)[None, :])
    pairs[LANES + c[:, None], r[None, :]] = ((c // nb)[:, None] == (r // MOBA_BLOCK)[None, :])
    place = np.zeros((nb, LANES, LANES), np.float32)
    for i in range(nb):
        for m in range(i):
            for n in range(nb):
                if m != n:
                    place[i, nb * m + n, n] = 1.0
    return (jnp.asarray(kfeat, BF16), jnp.asarray(pairs, BF16), jnp.asarray(place, BF16))


def _moba_kernel(slopes_ref, q_ref, k_ref, v_ref, kfeat_ref, pairs_ref, place_ref, o_ref, kaug_ref):
    hp = pl.program_id(1)
    t_len = q_ref.shape[0]
    nb = t_len // MOBA_BLOCK
    bs = MOBA_BLOCK

    k = k_ref[...]
    kaug_ref[:, 0:LANES] = k
    kaug_ref[:, LANES:2 * LANES] = kfeat_ref[...]
    kmean_t = _dot(pairs_ref[...], k) * (1.0 / bs)
    km = jnp.concatenate([kmean_t[0:LANES].T, kmean_t[LANES:2 * LANES].T], axis=1).astype(BF16)

    lane = lax.broadcasted_iota(jnp.int32, (bs, LANES), 1)
    lane2 = lax.broadcasted_iota(jnp.int32, (2 * bs, LANES), 1)
    rival_first = (lane2 // nb) < (lane2 % nb)
    s_row = lax.broadcasted_iota(jnp.int32, (2 * bs, bs), 0)
    s_col = lax.broadcasted_iota(jnp.int32, (2 * bs, bs), 1)
    causal = s_col <= (s_row % bs)
    slopes = (slopes_ref[2 * hp], slopes_ref[2 * hp + 1])
    scale = jnp.asarray(HEAD_DIM ** -0.5, BF16)

    for i in range(nb):
        q = q_ref[i * bs:(i + 1) * bs, :]
        zero = jnp.zeros_like(q)
        qs = jnp.concatenate([jnp.where(lane < 64, q, zero) * scale,
                              jnp.where(lane >= 64, q, zero) * scale], axis=0)
        gates = _dot(qs, km)
        g_n, g_m = gates[:, 0:LANES], gates[:, LANES:2 * LANES]
        beats = (g_m > g_n) | (rival_first & (g_m == g_n))
        rank = _dot(jnp.where(beats, 1.0, 0.0).astype(BF16), place_ref[i])
        chosen = (rank < MOBA_TOPK - 0.5) & (lane2 < i)
        feats = []
        for hh in range(2):
            picked = (slopes[hh] * bs) * (lane - i).astype(F32)
            other = jnp.where(lane == _OFF_LANE, slopes[hh], jnp.where(lane < i, NEG, 0.0))
            feats.append(jnp.where(chosen[hh * bs:(hh + 1) * bs], picked, other))
        q_aug = jnp.concatenate([qs, jnp.concatenate(feats, axis=0).astype(BF16)], axis=1)

        m_i = l_i = acc = None
        for j in [i] + list(range(i)):
            s = _dot_nt(q_aug, kaug_ref[j * bs:(j + 1) * bs, :])
            v_j = v_ref[j * bs:(j + 1) * bs, :]
            if j == i:
                s = jnp.where(causal, s, NEG)
                m_i = jnp.max(s, axis=1, keepdims=True)
                p = jnp.exp(s - m_i)
                l_i = jnp.sum(p, axis=1, keepdims=True)
                acc = _dot(p.astype(BF16), v_j)
            else:
                m_n = jnp.maximum(m_i, jnp.max(s, axis=1, keepdims=True))
                alpha = jnp.exp(m_i - m_n)
                p = jnp.exp(s - m_n)
                l_i = alpha * l_i + jnp.sum(p, axis=1, keepdims=True)
                acc = alpha * acc + _dot(p.astype(BF16), v_j)
                m_i = m_n
        o = acc / l_i
        o_ref[i * bs:(i + 1) * bs, :] = jnp.where(lane < 64, o[0:bs], o[bs:2 * bs]).astype(o_ref.dtype)


def _moba(slopes, q3, k3, v3):
    b, t, w = q3.shape
    kfeat, pairs, place = _moba_constants(t)
    spec = pl.BlockSpec((None, t, LANES), lambda bi, hp: (bi, 0, hp))
    const = lambda arr: pl.BlockSpec(arr.shape, lambda bi, hp: (0,) * arr.ndim)
    return pl.pallas_call(
        _moba_kernel,
        grid=(b, w // LANES),
        in_specs=[pl.BlockSpec(memory_space=pltpu.SMEM), spec, spec, spec,
                  const(kfeat), const(pairs), const(place)],
        out_specs=spec,
        out_shape=jax.ShapeDtypeStruct((b, t, w), BF16),
        scratch_shapes=[pltpu.VMEM((t, 2 * LANES), BF16)],
        compiler_params=pltpu.CompilerParams(
            dimension_semantics=("parallel", "arbitrary"), vmem_limit_bytes=VMEM_LIMIT),
        name="moba",
    )(slopes, q3, k3, v3, kfeat, pairs, place)


def _swa_kernel(slopes_ref, sinks_ref, q_ref, kd_ref, vd_ref, o_ref, kp_ref, vp_ref, bias_ref):
    w = SWA_WINDOW
    t_len = q_ref.shape[0]
    kp_ref[0:w, :] = jnp.zeros((w, kp_ref.shape[1]), BF16)
    vp_ref[0:w, :] = jnp.zeros((w, vp_ref.shape[1]), BF16)
    kp_ref[w:, :] = kd_ref[...]
    vp_ref[w:, :] = vd_ref[...]

    r = lax.broadcasted_iota(jnp.int32, (w, 2 * w), 0)
    c = lax.broadcasted_iota(jnp.int32, (w, 2 * w), 1)
    rel = w + r - c
    in_window = (rel >= 0) & (rel < w)
    rel_f = rel.astype(F32)
    for h in range(SWA_Q_HEADS):
        bias_ref[h] = jnp.where(in_window, -slopes_ref[h] * rel_f, NEG)
    lane_q = lax.broadcasted_iota(jnp.int32, (w, LANES), 1)

    def window(n, carry):
        qs = pl.multiple_of(n * w, w)
        not_before_start = (c >= w) | (n > 0)
        for slab in range(SWA_Q_HEADS // 2):
            kvh = slab // (SWA_Q_HEADS // SWA_KV_HEADS // 2)
            q = q_ref[pl.ds(qs, w), slab * LANES:(slab + 1) * LANES]
            kk = kp_ref[pl.ds(qs, 2 * w), kvh * LANES:(kvh + 1) * LANES]
            vv = vp_ref[pl.ds(qs, 2 * w), kvh * LANES:(kvh + 1) * LANES]
            outs = []
            for half in range(2):
                h = 2 * slab + half
                head_lane = (lane_q >= 64 * half) & (lane_q < 64 * half + 64)
                qh = jnp.where(head_lane, q, jnp.zeros_like(q)) * jnp.asarray(HEAD_DIM ** -0.5, BF16)
                s = _dot_nt(qh, kk) + bias_ref[h]
                s = jnp.where(not_before_start, s, NEG)
                sink = sinks_ref[h]
                m = jnp.maximum(jnp.max(s, axis=1, keepdims=True), sink)
                p = jnp.exp(s - m)
                l = jnp.sum(p, axis=1, keepdims=True) + jnp.exp(sink - m)
                outs.append(_dot(p.astype(BF16), vv) / l)
            o_ref[pl.ds(qs, w), slab * LANES:(slab + 1) * LANES] = (
                jnp.where(lane_q < 64, outs[0], outs[1]).astype(o_ref.dtype))
        return carry

    lax.fori_loop(0, t_len // w, window, 0)


def _swa(slopes, sinks, q3, kd3, vd3):
    b, t, wq = q3.shape
    wkv = kd3.shape[2]
    smem = pl.BlockSpec(memory_space=pltpu.SMEM)
    return pl.pallas_call(
        _swa_kernel,
        grid=(b,),
        in_specs=[smem, smem,
                  pl.BlockSpec((None, t, wq), lambda bi: (bi, 0, 0)),
                  pl.BlockSpec((None, t, wkv), lambda bi: (bi, 0, 0)),
                  pl.BlockSpec((None, t, wkv), lambda bi: (bi, 0, 0))],
        out_specs=pl.BlockSpec((None, t, wq), lambda bi: (bi, 0, 0)),
        out_shape=jax.ShapeDtypeStruct((b, t, wq), BF16),
        scratch_shapes=[pltpu.VMEM((t + SWA_WINDOW, wkv), BF16),
                        pltpu.VMEM((t + SWA_WINDOW, wkv), BF16),
                        pltpu.VMEM((SWA_Q_HEADS, SWA_WINDOW, 2 * SWA_WINDOW), F32)],
        compiler_params=pltpu.CompilerParams(
            dimension_semantics=("parallel",), vmem_limit_bytes=VMEM_LIMIT),
        name="swa",
    )(slopes, sinks, q3, kd3, vd3)


def _mem_kv_kernel(m_ref, g_ref, w_ref, o_ref):
    o_ref[...] = _dot(_rms(m_ref[...], g_ref[...]).astype(BF16), w_ref[...]).astype(o_ref.dtype)


def _mem_kv(mem2d, g, w_kv):
    n = mem2d.shape[0]
    tm = MEM_TILE
    wout = w_kv.shape[1]
    return pl.pallas_call(
        _mem_kv_kernel,
        grid=(n // tm,),
        in_specs=[pl.BlockSpec((tm, D_MODEL), lambda i: (i, 0)),
                  pl.BlockSpec((1, D_MODEL), lambda i: (0, 0)),
                  pl.BlockSpec((D_MODEL, wout), lambda i: (0, 0))],
        out_specs=pl.BlockSpec((tm, wout), lambda i: (i, 0)),
        out_shape=jax.ShapeDtypeStruct((n, wout), BF16),
        compiler_params=pltpu.CompilerParams(
            dimension_semantics=("parallel",), vmem_limit_bytes=VMEM_LIMIT),
        name="mem_kv",
    )(mem2d, g, w_kv)


_GROUP_LANE0 = N_GROUPS * EXPERTS_PER_GROUP
_LOW = -3.0e38


def _route(logits):
    lane = lax.broadcasted_iota(jnp.int32, logits.shape, 1)
    lane_f = lane.astype(F32)
    is_g = (lane >= _GROUP_LANE0) & (lane < _GROUP_LANE0 + N_GROUPS)
    gl = jnp.where(is_g, logits, _LOW)
    gmax = jnp.max(gl, axis=1, keepdims=True)
    g_p = 1.0 / jnp.sum(jnp.exp(gl - gmax), axis=1, keepdims=True)
    g_idx = jnp.min(jnp.where(gl == gmax, lane_f - _GROUP_LANE0, 1e9), axis=1, keepdims=True)
    in_group = (lane < _GROUP_LANE0) & ((lane // EXPERTS_PER_GROUP).astype(F32) == g_idx)
    el = jnp.where(in_group, logits, _LOW)
    m1 = jnp.max(el, axis=1, keepdims=True)
    i1 = jnp.min(jnp.where(el == m1, lane_f, 1e9), axis=1, keepdims=True)
    el2 = jnp.where(lane_f == i1, _LOW, el)
    m2 = jnp.max(el2, axis=1, keepdims=True)
    i2 = jnp.min(jnp.where(el2 == m2, lane_f, 1e9), axis=1, keepdims=True)
    t = jnp.exp(m2 - m1)
    w1 = g_p / (1.0 + t)
    w2 = w1 * t
    return jnp.where(lane_f == i1, w1, jnp.where(lane_f == i2, w2, 0.0))


def _mix_kernel(x_ref, a_ref, b_ref, ga_ref, gb_ref, kv_ref,
                wa_ref, wb_ref, wo_ref, g2_ref, wq_ref, wxo_ref, g3_ref,
                wrh_ref, wrl_ref, br_ref,
                x2_ref, h3_ref, comb_ref):
    pa = _dot(a_ref[...], wa_ref[...])
    pb = _dot(b_ref[...], wb_ref[...])
    merged = (jax.nn.sigmoid(ga_ref[...].astype(F32)) * pa
              + jax.nn.sigmoid(gb_ref[...].astype(F32)) * pb)
    x1 = x_ref[...] + _dot(merged.astype(BF16), wo_ref[...])

    q = _dot(_rms(x1, g2_ref[...]).astype(BF16), wq_ref[...]).astype(BF16)
    width = XATTN_HEADS * XATTN_HEAD_DIM
    heads = []
    for hd in range(XATTN_HEADS):
        lo = hd * XATTN_HEAD_DIM
        kx = kv_ref[:, lo:lo + XATTN_HEAD_DIM]
        vx = kv_ref[:, width + lo:width + lo + XATTN_HEAD_DIM]
        s = _dot_nt(q[:, lo:lo + XATTN_HEAD_DIM], kx) * (XATTN_HEAD_DIM ** -0.5)
        m = jnp.max(s, axis=1, keepdims=True)
        p = jnp.exp(s - m)
        l = jnp.sum(p, axis=1, keepdims=True)
        heads.append((_dot(p.astype(BF16), vx) / l).astype(BF16))
    x2 = x1 + _dot(jnp.concatenate(heads, axis=1), wxo_ref[...])
    x2_ref[...] = x2

    h3 = _rms(x2, g3_ref[...])
    h3_hi = h3.astype(BF16)
    h3_lo = (h3 - h3_hi.astype(F32)).astype(BF16)
    h3_ref[...] = h3_hi
    logits = (_dot(h3_hi, wrh_ref[...])
              + (_dot(h3_hi, wrl_ref[...]) + _dot(h3_lo, wrh_ref[...]))
              + br_ref[...])
    comb_ref[...] = _route(logits)


def _mix(x2d, a, b, ga, gb, kv3, wa, wb, wo, g2, wq, wxo, g3, wrh, wrl, br, seq_len):
    n = x2d.shape[0]
    tm = MIX_TILE
    tiles_per_seq = seq_len // tm
    row = lambda w: pl.BlockSpec((tm, w), lambda i: (i, 0))
    full = lambda arr: pl.BlockSpec(arr.shape, lambda i: (0,) * arr.ndim)
    mem_len, kvw = kv3.shape[1], kv3.shape[2]
    return pl.pallas_call(
        _mix_kernel,
        grid=(n // tm,),
        in_specs=[row(D_MODEL), row(a.shape[1]), row(b.shape[1]), row(D_MODEL), row(D_MODEL),
                  pl.BlockSpec((None, mem_len, kvw), lambda i: (i // tiles_per_seq, 0, 0)),
                  full(wa), full(wb), full(wo), full(g2), full(wq), full(wxo), full(g3),
                  full(wrh), full(wrl), full(br)],
        out_specs=[row(D_MODEL), row(D_MODEL), row(LANES)],
        out_shape=[jax.ShapeDtypeStruct((n, D_MODEL), F32),
                   jax.ShapeDtypeStruct((n, D_MODEL), BF16),
                   jax.ShapeDtypeStruct((n, LANES), F32)],
        compiler_params=pltpu.CompilerParams(
            dimension_semantics=("parallel",), vmem_limit_bytes=VMEM_LIMIT),
        name="mix",
    )(x2d, a, b, ga, gb, kv3, wa, wb, wo, g2, wq, wxo, g3, wrh, wrl, br)


def _dot_tn(a, b):
    return lax.dot_general(a, b, (((0,), (0,)), ((), ())), preferred_element_type=F32)


def _moe_kernel(h_ref, comb_ref, x_ref, wg_ref, wu_ref, wd_ref, gf_ref, o_ref, tri_ref, hid_ref):
    g = pl.program_id(1)
    tm = h_ref.shape[0]
    ch = MOE_CHUNK

    @pl.when(g == 0)
    def _():
        r = lax.broadcasted_iota(jnp.int32, (tm, tm), 0)
        c = lax.broadcasted_iota(jnp.int32, (tm, tm), 1)
        tri_ref[...] = jnp.where(c < r, 1.0, 0.0).astype(BF16)
        o_ref[...] = jnp.zeros(o_ref.shape, o_ref.dtype)

    comb = comb_ref[...]
    lane = lax.broadcasted_iota(jnp.int32, comb.shape, 1)
    lane_f = lane.astype(F32)
    lo = g * EXPERTS_PER_GROUP
    routed = jnp.where((comb > 0.0) & (lane >= lo) & (lane < lo + EXPERTS_PER_GROUP), 1.0, 0.0)
    member = _dot(routed.astype(BF16), jnp.ones((LANES, LANES), BF16)) > 0.5
    rank = _dot(tri_ref[...], jnp.where(member, 1.0, 0.0).astype(BF16))
    pos = jnp.where(member, rank, -1.0)
    n_chunks = (jnp.max(pos).astype(jnp.int32) + ch) // ch

    c1 = comb.astype(BF16)
    r1 = comb - c1.astype(F32)
    c2 = r1.astype(BF16)
    c3 = (r1 - c2.astype(F32)).astype(BF16)
    comb_terms = jnp.concatenate([c1, c2, c3], axis=1)

    def chunk(ci, carry):
        base = (ci * ch).astype(F32)
        onehot = jnp.where(pos == base + lane_f, 1.0, 0.0).astype(BF16)
        hc = _dot_tn(onehot, h_ref[...]).astype(BF16)
        ct = _dot_tn(onehot, comb_terms)
        cc = (ct[:, 0:LANES] + ct[:, LANES:2 * LANES]) + ct[:, 2 * LANES:3 * LANES]
        lane_c = lax.broadcasted_iota(jnp.int32, cc.shape, 1)
        for e in range(EXPERTS_PER_GROUP):
            gate = _dot(hc, wg_ref[e])
            up = _dot(hc, wu_ref[e])
            c = jnp.sum(jnp.where(lane_c == lo + e, cc, 0.0), axis=1, keepdims=True)
            hid_ref[:, e * D_EXPERT:(e + 1) * D_EXPERT] = (
                gate * jax.nn.sigmoid(gate) * up * c).astype(BF16)
        y = _dot(hid_ref[...], wd_ref[...])
        y_hi = y.astype(BF16)
        y_lo = (y - y_hi.astype(F32)).astype(BF16)
        o_ref[...] += _dot(jnp.concatenate([onehot, onehot], axis=1),
                           jnp.concatenate([y_hi, y_lo], axis=0))
        return carry

    lax.fori_loop(0, n_chunks, chunk, 0)

    @pl.when(g == N_GROUPS - 1)
    def _():
        o_ref[...] = _rms(x_ref[...] + o_ref[...], gf_ref[...])


def _moe(h3, comb, x2, wg, wu, wd, gf):
    n = h3.shape[0]
    tm = MOE_TILE
    return pl.pallas_call(
        _moe_kernel,
        grid=(n // tm, N_GROUPS),
        in_specs=[pl.BlockSpec((tm, D_MODEL), lambda i, g: (i, 0)),
                  pl.BlockSpec((tm, LANES), lambda i, g: (i, 0)),
                  pl.BlockSpec((tm, D_MODEL), lambda i, g: (i, 0)),
                  pl.BlockSpec((None, EXPERTS_PER_GROUP, D_MODEL, D_EXPERT), lambda i, g: (g, 0, 0, 0)),
                  pl.BlockSpec((None, EXPERTS_PER_GROUP, D_MODEL, D_EXPERT), lambda i, g: (g, 0, 0, 0)),
                  pl.BlockSpec((None, EXPERTS_PER_GROUP * D_EXPERT, D_MODEL), lambda i, g: (g, 0, 0)),
                  pl.BlockSpec((1, D_MODEL), lambda i, g: (0, 0))],
        out_specs=pl.BlockSpec((tm, D_MODEL), lambda i, g: (i, 0)),
        out_shape=jax.ShapeDtypeStruct((n, D_MODEL), F32),
        scratch_shapes=[pltpu.VMEM((tm, tm), BF16),
                        pltpu.VMEM((MOE_CHUNK, EXPERTS_PER_GROUP * D_EXPERT), BF16)],
        compiler_params=pltpu.CompilerParams(
            dimension_semantics=("parallel", "arbitrary"), vmem_limit_bytes=VMEM_LIMIT),
        name="moe",
    )(h3, comb, x2, wg, wu, wd, gf)


def _alibi_slopes(n_heads):
    return jnp.asarray([2.0 ** (-8.0 * (i + 1) / n_heads) for i in range(n_heads)], dtype=F32)


def _split_bf16(w):
    hi = w.astype(BF16)
    return hi, (w - hi.astype(F32)).astype(BF16)


def kernel(x, mem, mix_norm_g, w_in, moba_out_w, swa_out_w, swa_sinks, mix_out_w, xattn_norm_g, mem_norm_g, xattn_wq, xattn_wkv, xattn_wo, ffn_norm_g, router_group_w, router_group_b, router_expert_w, router_expert_b, expert_w_gate, expert_w_up, expert_w_down, final_norm_g):
    bsz, t, d = x.shape
    n = bsz * t
    assert w_in.shape[0] == 1, "the final RMSNorm is fused into the (single) layer's MoE kernel"
    l = 0
    x2d = x.reshape(n, d)
    mem2d = mem.reshape(-1, d)
    mem_len = mem.shape[1]
    moba_slopes = _alibi_slopes(MOBA_HEADS)
    swa_slopes = _alibi_slopes(SWA_Q_HEADS)

    w = w_in[l]
    c = 3 * MOBA_HEADS * HEAD_DIM + SWA_Q_HEADS * HEAD_DIM
    kvw = SWA_KV_HEADS * HEAD_DIM
    dup = lambda m: jnp.concatenate(
        [m[:, j * HEAD_DIM:(j + 1) * HEAD_DIM] for j in range(SWA_KV_HEADS) for _ in range(2)], axis=1)
    w_all = jnp.concatenate(
        [w[:, :c], dup(w[:, c:c + kvw]), dup(w[:, c + kvw:c + 2 * kvw]), w[:, c + 2 * kvw:]],
        axis=1).astype(BF16)
    qa, ka, va, qb, kd, vd, ga, gb = _in_proj(x2d, mix_norm_g[l].reshape(1, d), w_all)

    r3 = lambda arr: arr.reshape(bsz, t, arr.shape[1])
    a = _moba(moba_slopes, r3(qa), r3(ka), r3(va)).reshape(n, -1)
    b = _swa(swa_slopes, swa_sinks[l], r3(qb), r3(kd), r3(vd)).reshape(n, -1)

    kv = _mem_kv(mem2d, mem_norm_g[l].reshape(1, d), xattn_wkv[l].astype(BF16))
    kv3 = kv.reshape(bsz, mem_len, kv.shape[1])

    pad = LANES - _GROUP_LANE0 - N_GROUPS
    w_r = jnp.concatenate([router_expert_w[l], router_group_w[l], jnp.zeros((d, pad), F32)], axis=1)
    b_r = jnp.concatenate([router_expert_b[l], router_group_b[l], jnp.zeros((pad,), F32)]).reshape(1, LANES)
    wrh, wrl = _split_bf16(w_r)
    x2, h3, comb = _mix(
        x2d, a, b, ga, gb, kv3,
        moba_out_w[l].astype(BF16), swa_out_w[l].astype(BF16), mix_out_w[l].astype(BF16),
        xattn_norm_g[l].reshape(1, d), xattn_wq[l].astype(BF16), xattn_wo[l].astype(BF16),
        ffn_norm_g[l].reshape(1, d), wrh, wrl, b_r, t)

    wd = expert_w_down[l].astype(BF16).reshape(N_GROUPS, EXPERTS_PER_GROUP * D_EXPERT, d)
    out = _moe(h3, comb, x2, expert_w_gate[l].astype(BF16), expert_w_up[l].astype(BF16), wd,
               final_norm_g.reshape(1, d))
    return out.reshape(bsz, t, d)
```

```python
import numpy as np

import jax
import jax.numpy as jnp
from jax import lax
from jax.experimental import pallas as pl
from jax.experimental.pallas import tpu as pltpu

F32 = jnp.float32
BF16 = jnp.bfloat16

D_MODEL = 1024
HEAD_DIM = 64
MOBA_HEADS = 8
MOBA_BLOCK = 256
MOBA_TOPK = 3
SWA_Q_HEADS = 8
SWA_KV_HEADS = 2
SWA_WINDOW = 128
XATTN_HEADS = 4
XATTN_HEAD_DIM = 128
N_GROUPS = 4
EXPERTS_PER_GROUP = 8
D_EXPERT = 256
RMS_EPS = 1e-6
NEG = -1e30
LANES = 128
VMEM_LIMIT = 56 * 1024 * 1024

IN_TILE = 512
MIX_TILE = 512
MIX_SLAB = 256
MOE_TILE = 1024
MOE_CHUNK = LANES
MEM_TILE = 512


def _rms(x, g):
    return x * lax.rsqrt(jnp.mean(x * x, axis=-1, keepdims=True) + RMS_EPS) * g


def _dot(a, b):
    return jnp.dot(a, b, preferred_element_type=F32)


def _dot_nt(a, b):
    return lax.dot_general(a, b, (((1,), (1,)), ((), ())), preferred_element_type=F32)


_IN_WIDTHS = (512, 512, 512, 512, 256, 256, 1024, 1024)


def _in_proj_kernel(x_ref, g_ref, w_ref, *out_refs):
    h = _rms(x_ref[...], g_ref[...]).astype(BF16)
    c0 = 0
    for o_ref, width in zip(out_refs, _IN_WIDTHS):
        o_ref[...] = _dot(h, w_ref[:, c0:c0 + width]).astype(o_ref.dtype)
        c0 += width


def _in_proj(x2d, g, w_all):
    n = x2d.shape[0]
    tm = IN_TILE
    wtot = w_all.shape[1]
    return pl.pallas_call(
        _in_proj_kernel,
        grid=(n // tm,),
        in_specs=[
            pl.BlockSpec((tm, D_MODEL), lambda i: (i, 0)),
            pl.BlockSpec((1, D_MODEL), lambda i: (0, 0)),
            pl.BlockSpec((D_MODEL, wtot), lambda i: (0, 0)),
        ],
        out_specs=[pl.BlockSpec((tm, w), lambda i: (i, 0)) for w in _IN_WIDTHS],
        out_shape=[jax.ShapeDtypeStruct((n, w), BF16) for w in _IN_WIDTHS],
        compiler_params=pltpu.CompilerParams(
            dimension_semantics=("parallel",), vmem_limit_bytes=VMEM_LIMIT),
        name="in_proj",
    )(x2d, g, w_all)


_OFF_LANE = 8


def _moba_constants(t_len):
    nb = t_len // MOBA_BLOCK
    r = np.arange(t_len)
    kfeat = np.zeros((t_len, LANES), np.float32)
    kfeat[r, r // MOBA_BLOCK] = 1.0
    kfeat[:, _OFF_LANE] = r % MOBA_BLOCK
    c = np.arange(nb * nb)
    pairs = np.zeros((2 * LANES, t_len), np.float32)
    pairs[c[:, None], r[None, :]] = ((c % nb)[:, None] == (r // MOBA_BLOCK)[None, :])
    pairs[LANES + c[:, None], r[None, :]] = ((c // nb)[:, None] == (r // MOBA_BLOCK)[None, :])
    place = np.zeros((nb, LANES, LANES), np.float32)
    for i in range(nb):
        for m in range(i):
            for n in range(nb):
                if m != n:
                    place[i, nb * m + n, n] = 1.0
    return (jnp.asarray(kfeat, BF16), jnp.asarray(pairs, BF16), jnp.asarray(place, BF16))


def _moba_kernel(slopes_ref, q_ref, k_ref, v_ref, kfeat_ref, pairs_ref, place_ref, o_ref, kaug_ref):
    hp = pl.program_id(1)
    t_len = q_ref.shape[0]
    nb = t_len // MOBA_BLOCK
    bs = MOBA_BLOCK

    k = k_ref[...]
    kaug_ref[:, 0:LANES] = k
    kaug_ref[:, LANES:2 * LANES] = kfeat_ref[...]
    kmean_t = _dot(pairs_ref[...], k) * (1.0 / bs)
    km = jnp.concatenate([kmean_t[0:LANES].T, kmean_t[LANES:2 * LANES].T], axis=1).astype(BF16)

    lane = lax.broadcasted_iota(jnp.int32, (bs, LANES), 1)
    lane2 = lax.broadcasted_iota(jnp.int32, (2 * bs, LANES), 1)
    rival_first = (lane2 // nb) < (lane2 % nb)
    s_row = lax.broadcasted_iota(jnp.int32, (2 * bs, bs), 0)
    s_col = lax.broadcasted_iota(jnp.int32, (2 * bs, bs), 1)
    causal = s_col <= (s_row % bs)
    slopes = (slopes_ref[2 * hp], slopes_ref[2 * hp + 1])
    scale = jnp.asarray(HEAD_DIM ** -0.5, BF16)

    for i in range(nb):
        q = q_ref[i * bs:(i + 1) * bs, :]
        zero = jnp.zeros_like(q)
        qs = jnp.concatenate([jnp.where(lane < 64, q, zero) * scale,
                              jnp.where(lane >= 64, q, zero) * scale], axis=0)
        gates = _dot(qs, km)
        g_n, g_m = gates[:, 0:LANES], gates[:, LANES:2 * LANES]
        beats = (g_m > g_n) | (rival_first & (g_m == g_n))
        rank = _dot(jnp.where(beats, 1.0, 0.0).astype(BF16), place_ref[i])
        chosen = (rank < MOBA_TOPK - 0.5) & (lane2 < i)
        feats = []
        for hh in range(2):
            picked = (slopes[hh] * bs) * (lane - i).astype(F32)
            other = jnp.where(lane == _OFF_LANE, slopes[hh], jnp.where(lane < i, NEG, 0.0))
            feats.append(jnp.where(chosen[hh * bs:(hh + 1) * bs], picked, other))
        q_aug = jnp.concatenate([qs, jnp.concatenate(feats, axis=0).astype(BF16)], axis=1)

        m_i = l_i = acc = None
        for j in [i] + list(range(i)):
            s = _dot_nt(q_aug, kaug_ref[j * bs:(j + 1) * bs, :])
            v_j = v_ref[j * bs:(j + 1) * bs, :]
            if j == i:
                s = jnp.where(causal, s, NEG)
                m_i = jnp.max(s, axis=1, keepdims=True)
                p = jnp.exp(s - m_i)
                l_i = jnp.sum(p, axis=1, keepdims=True)
                acc = _dot(p.astype(BF16), v_j)
            else:
                m_n = jnp.maximum(m_i, jnp.max(s, axis=1, keepdims=True))
                alpha = jnp.exp(m_i - m_n)
                p = jnp.exp(s - m_n)
                l_i = alpha * l_i + jnp.sum(p, axis=1, keepdims=True)
                acc = alpha * acc + _dot(p.astype(BF16), v_j)
                m_i = m_n
        o = acc / l_i
        o_ref[i * bs:(i + 1) * bs, :] = jnp.where(lane < 64, o[0:bs], o[bs:2 * bs]).astype(o_ref.dtype)


def _moba(slopes, q3, k3, v3):
    b, t, w = q3.shape
    kfeat, pairs, place = _moba_constants(t)
    spec = pl.BlockSpec((None, t, LANES), lambda bi, hp: (bi, 0, hp))
    const = lambda arr: pl.BlockSpec(arr.shape, lambda bi, hp: (0,) * arr.ndim)
    return pl.pallas_call(
        _moba_kernel,
        grid=(b, w // LANES),
        in_specs=[pl.BlockSpec(memory_space=pltpu.SMEM), spec, spec, spec,
                  const(kfeat), const(pairs), const(place)],
        out_specs=spec,
        out_shape=jax.ShapeDtypeStruct((b, t, w), BF16),
        scratch_shapes=[pltpu.VMEM((t, 2 * LANES), BF16)],
        compiler_params=pltpu.CompilerParams(
            dimension_semantics=("parallel", "arbitrary"), vmem_limit_bytes=VMEM_LIMIT),
        name="moba",
    )(slopes, q3, k3, v3, kfeat, pairs, place)


def _swa_kernel(slopes_ref, sinks_ref, q_ref, kd_ref, vd_ref, o_ref, kp_ref, vp_ref, bias_ref):
    w = SWA_WINDOW
    t_len = q_ref.shape[0]
    kp_ref[0:w, :] = jnp.zeros((w, kp_ref.shape[1]), BF16)
    vp_ref[0:w, :] = jnp.zeros((w, vp_ref.shape[1]), BF16)
    kp_ref[w:, :] = kd_ref[...]
    vp_ref[w:, :] = vd_ref[...]

    r = lax.broadcasted_iota(jnp.int32, (w, 2 * w), 0)
    c = lax.broadcasted_iota(jnp.int32, (w, 2 * w), 1)
    rel = w + r - c
    in_window = (rel >= 0) & (rel < w)
    rel_f = rel.astype(F32)
    for h in range(SWA_Q_HEADS):
        bias_ref[h] = jnp.where(in_window, -slopes_ref[h] * rel_f, NEG)
    lane_q = lax.broadcasted_iota(jnp.int32, (w, LANES), 1)
    scale = jnp.asarray(HEAD_DIM ** -0.5, BF16)

    for n in range(t_len // w):
        qs = n * w
        for slab in range(SWA_Q_HEADS // 2):
            kvh = slab // (SWA_Q_HEADS // SWA_KV_HEADS // 2)
            q = q_ref[qs:qs + w, slab * LANES:(slab + 1) * LANES]
            kk = kp_ref[qs:qs + 2 * w, kvh * LANES:(kvh + 1) * LANES]
            vv = vp_ref[qs:qs + 2 * w, kvh * LANES:(kvh + 1) * LANES]
            outs = []
            for half in range(2):
                h = 2 * slab + half
                head_lane = (lane_q >= 64 * half) & (lane_q < 64 * half + 64)
                qh = jnp.where(head_lane, q, jnp.zeros_like(q)) * scale
                s = _dot_nt(qh, kk) + bias_ref[h]
                if n == 0:
                    s = jnp.where(c >= w, s, NEG)
                sink = sinks_ref[h]
                m = jnp.maximum(jnp.max(s, axis=1, keepdims=True), sink)
                p = jnp.exp(s - m)
                l = jnp.sum(p, axis=1, keepdims=True) + jnp.exp(sink - m)
                outs.append(_dot(p.astype(BF16), vv) / l)
            o_ref[qs:qs + w, slab * LANES:(slab + 1) * LANES] = (
                jnp.where(lane_q < 64, outs[0], outs[1]).astype(o_ref.dtype))


def _swa(slopes, sinks, q3, kd3, vd3):
    b, t, wq = q3.shape
    wkv = kd3.shape[2]
    smem = pl.BlockSpec(memory_space=pltpu.SMEM)
    return pl.pallas_call(
        _swa_kernel,
        grid=(b,),
        in_specs=[smem, smem,
                  pl.BlockSpec((None, t, wq), lambda bi: (bi, 0, 0)),
                  pl.BlockSpec((None, t, wkv), lambda bi: (bi, 0, 0)),
                  pl.BlockSpec((None, t, wkv), lambda bi: (bi, 0, 0))],
        out_specs=pl.BlockSpec((None, t, wq), lambda bi: (bi, 0, 0)),
        out_shape=jax.ShapeDtypeStruct((b, t, wq), BF16),
        scratch_shapes=[pltpu.VMEM((t + SWA_WINDOW, wkv), BF16),
                        pltpu.VMEM((t + SWA_WINDOW, wkv), BF16),
                        pltpu.VMEM((SWA_Q_HEADS, SWA_WINDOW, 2 * SWA_WINDOW), F32)],
        compiler_params=pltpu.CompilerParams(
            dimension_semantics=("parallel",), vmem_limit_bytes=VMEM_LIMIT),
        name="swa",
    )(slopes, sinks, q3, kd3, vd3)


def _mem_kv_kernel(m_ref, g_ref, w_ref, o_ref):
    o_ref[...] = _dot(_rms(m_ref[...], g_ref[...]).astype(BF16), w_ref[...]).astype(o_ref.dtype)


def _mem_kv(mem2d, g, w_kv):
    n = mem2d.shape[0]
    tm = MEM_TILE
    wout = w_kv.shape[1]
    return pl.pallas_call(
        _mem_kv_kernel,
        grid=(n // tm,),
        in_specs=[pl.BlockSpec((tm, D_MODEL), lambda i: (i, 0)),
                  pl.BlockSpec((1, D_MODEL), lambda i: (0, 0)),
                  pl.BlockSpec((D_MODEL, wout), lambda i: (0, 0))],
        out_specs=pl.BlockSpec((tm, wout), lambda i: (i, 0)),
        out_shape=jax.ShapeDtypeStruct((n, wout), BF16),
        compiler_params=pltpu.CompilerParams(
            dimension_semantics=("parallel",), vmem_limit_bytes=VMEM_LIMIT),
        name="mem_kv",
    )(mem2d, g, w_kv)


_GROUP_LANE0 = N_GROUPS * EXPERTS_PER_GROUP
_LOW = -3.0e38


def _route(logits):
    lane = lax.broadcasted_iota(jnp.int32, logits.shape, 1)
    lane_f = lane.astype(F32)
    is_g = (lane >= _GROUP_LANE0) & (lane < _GROUP_LANE0 + N_GROUPS)
    gl = jnp.where(is_g, logits, _LOW)
    gmax = jnp.max(gl, axis=1, keepdims=True)
    g_p = 1.0 / jnp.sum(jnp.exp(gl - gmax), axis=1, keepdims=True)
    g_idx = jnp.min(jnp.where(gl == gmax, lane_f - _GROUP_LANE0, 1e9), axis=1, keepdims=True)
    in_group = (lane < _GROUP_LANE0) & ((lane // EXPERTS_PER_GROUP).astype(F32) == g_idx)
    el = jnp.where(in_group, logits, _LOW)
    m1 = jnp.max(el, axis=1, keepdims=True)
    i1 = jnp.min(jnp.where(el == m1, lane_f, 1e9), axis=1, keepdims=True)
    el2 = jnp.where(lane_f == i1, _LOW, el)
    m2 = jnp.max(el2, axis=1, keepdims=True)
    i2 = jnp.min(jnp.where(el2 == m2, lane_f, 1e9), axis=1, keepdims=True)
    t = jnp.exp(m2 - m1)
    w1 = g_p / (1.0 + t)
    w2 = w1 * t
    return jnp.where(lane_f == i1, w1, jnp.where(lane_f == i2, w2, 0.0))


def _mix_kernel(x_ref, a_ref, b_ref, ga_ref, gb_ref, kv_ref,
                wa_ref, wb_ref, wo_ref, g2_ref, wq_ref, wxo_ref, g3_ref,
                wrh_ref, wrl_ref, br_ref,
                x2_ref, h3_ref, comb_ref):
    for r0 in range(0, x_ref.shape[0], MIX_SLAB):
        rows = slice(r0, r0 + MIX_SLAB)
        _mix_rows(rows, x_ref, a_ref, b_ref, ga_ref, gb_ref, kv_ref,
                  wa_ref, wb_ref, wo_ref, g2_ref, wq_ref, wxo_ref, g3_ref,
                  wrh_ref, wrl_ref, br_ref, x2_ref, h3_ref, comb_ref)


def _mix_rows(rows, x_ref, a_ref, b_ref, ga_ref, gb_ref, kv_ref,
              wa_ref, wb_ref, wo_ref, g2_ref, wq_ref, wxo_ref, g3_ref,
              wrh_ref, wrl_ref, br_ref, x2_ref, h3_ref, comb_ref):
    pa = _dot(a_ref[rows, :], wa_ref[...])
    pb = _dot(b_ref[rows, :], wb_ref[...])
    merged = (jax.nn.sigmoid(ga_ref[rows, :].astype(F32)) * pa
              + jax.nn.sigmoid(gb_ref[rows, :].astype(F32)) * pb)
    x1 = x_ref[rows, :] + _dot(merged.astype(BF16), wo_ref[...])

    q = _dot(_rms(x1, g2_ref[...]).astype(BF16), wq_ref[...]).astype(BF16)
    width = XATTN_HEADS * XATTN_HEAD_DIM
    heads = []
    for hd in range(XATTN_HEADS):
        lo = hd * XATTN_HEAD_DIM
        kx = kv_ref[:, lo:lo + XATTN_HEAD_DIM]
        vx = kv_ref[:, width + lo:width + lo + XATTN_HEAD_DIM]
        s = _dot_nt(q[:, lo:lo + XATTN_HEAD_DIM], kx) * (XATTN_HEAD_DIM ** -0.5)
        m = jnp.max(s, axis=1, keepdims=True)
        p = jnp.exp(s - m)
        l = jnp.sum(p, axis=1, keepdims=True)
        heads.append((_dot(p.astype(BF16), vx) / l).astype(BF16))
    x2 = x1 + _dot(jnp.concatenate(heads, axis=1), wxo_ref[...])
    x2_ref[rows, :] = x2

    h3 = _rms(x2, g3_ref[...])
    h3_hi = h3.astype(BF16)
    h3_lo = (h3 - h3_hi.astype(F32)).astype(BF16)
    h3_ref[rows, :] = h3_hi
    hi_terms = _dot(h3_hi, jnp.concatenate([wrh_ref[...], wrl_ref[...]], axis=1))
    logits = (hi_terms[:, 0:LANES]
              + (hi_terms[:, LANES:2 * LANES] + _dot(h3_lo, wrh_ref[...]))
              + br_ref[...])
    comb_ref[rows, :] = _route(logits)


def _mix(x2d, a, b, ga, gb, kv3, wa, wb, wo, g2, wq, wxo, g3, wrh, wrl, br, seq_len):
    n = x2d.shape[0]
    tm = MIX_TILE
    tiles_per_seq = seq_len // tm
    row = lambda w: pl.BlockSpec((tm, w), lambda i: (i, 0))
    full = lambda arr: pl.BlockSpec(arr.shape, lambda i: (0,) * arr.ndim)
    mem_len, kvw = kv3.shape[1], kv3.shape[2]
    return pl.pallas_call(
        _mix_kernel,
        grid=(n // tm,),
        in_specs=[row(D_MODEL), row(a.shape[1]), row(b.shape[1]), row(D_MODEL), row(D_MODEL),
                  pl.BlockSpec((None, mem_len, kvw), lambda i: (i // tiles_per_seq, 0, 0)),
                  full(wa), full(wb), full(wo), full(g2), full(wq), full(wxo), full(g3),
                  full(wrh), full(wrl), full(br)],
        out_specs=[row(D_MODEL), row(D_MODEL), row(LANES)],
        out_shape=[jax.ShapeDtypeStruct((n, D_MODEL), F32),
                   jax.ShapeDtypeStruct((n, D_MODEL), BF16),
                   jax.ShapeDtypeStruct((n, LANES), F32)],
        compiler_params=pltpu.CompilerParams(
            dimension_semantics=("parallel",), vmem_limit_bytes=VMEM_LIMIT),
        name="mix",
    )(x2d, a, b, ga, gb, kv3, wa, wb, wo, g2, wq, wxo, g3, wrh, wrl, br)


def _dot_tn(a, b):
    return lax.dot_general(a, b, (((0,), (0,)), ((), ())), preferred_element_type=F32)


def _moe_kernel(h_ref, comb_ref, x_ref, wg_ref, wu_ref, wd_ref, gf_ref, o_ref, tri_ref, hid_ref):
    g = pl.program_id(1)
    tm = h_ref.shape[0]
    ch = MOE_CHUNK

    @pl.when(g == 0)
    def _():
        r = lax.broadcasted_iota(jnp.int32, (tm, tm), 0)
        c = lax.broadcasted_iota(jnp.int32, (tm, tm), 1)
        tri_ref[...] = jnp.where(c < r, 1.0, 0.0).astype(BF16)
        o_ref[...] = jnp.zeros(o_ref.shape, o_ref.dtype)

    comb = comb_ref[...]
    lane = lax.broadcasted_iota(jnp.int32, comb.shape, 1)
    lane_f = lane.astype(F32)
    lo = g * EXPERTS_PER_GROUP
    routed = jnp.where((comb > 0.0) & (lane >= lo) & (lane < lo + EXPERTS_PER_GROUP), 1.0, 0.0)
    member = _dot(routed.astype(BF16), jnp.ones((LANES, LANES), BF16)) > 0.5
    rank = _dot(tri_ref[...], jnp.where(member, 1.0, 0.0).astype(BF16))
    pos = jnp.where(member, rank, -1.0)
    n_chunks = (jnp.max(pos).astype(jnp.int32) + ch) // ch

    c1 = comb.astype(BF16)
    r1 = comb - c1.astype(F32)
    c2 = r1.astype(BF16)
    c3 = (r1 - c2.astype(F32)).astype(BF16)
    comb_terms = jnp.concatenate([c1, c2, c3], axis=1)

    def chunk(ci, carry):
        base = (ci * ch).astype(F32)
        onehot = jnp.where(pos == base + lane_f, 1.0, 0.0).astype(BF16)
        hc = _dot_tn(onehot, h_ref[...]).astype(BF16)
        ct = _dot_tn(onehot, comb_terms)
        cc = (ct[:, 0:LANES] + ct[:, LANES:2 * LANES]) + ct[:, 2 * LANES:3 * LANES]
        lane_c = lax.broadcasted_iota(jnp.int32, cc.shape, 1)
        for e in range(EXPERTS_PER_GROUP):
            gate = _dot(hc, wg_ref[e])
            up = _dot(hc, wu_ref[e])
            c = jnp.sum(jnp.where(lane_c == lo + e, cc, 0.0), axis=1, keepdims=True)
            hid_ref[:, e * D_EXPERT:(e + 1) * D_EXPERT] = (
                gate * jax.nn.sigmoid(gate) * up * c).astype(BF16)
        y = _dot(hid_ref[...], wd_ref[...])
        y_hi = y.astype(BF16)
        y_lo = (y - y_hi.astype(F32)).astype(BF16)
        o_ref[...] += _dot(jnp.concatenate([onehot, onehot], axis=1),
                           jnp.concatenate([y_hi, y_lo], axis=0))
        return carry

    lax.fori_loop(0, n_chunks, chunk, 0)

    @pl.when(g == N_GROUPS - 1)
    def _():
        o_ref[...] = _rms(x_ref[...] + o_ref[...], gf_ref[...])


def _moe(h3, comb, x2, wg, wu, wd, gf):
    n = h3.shape[0]
    tm = MOE_TILE
    return pl.pallas_call(
        _moe_kernel,
        grid=(n // tm, N_GROUPS),
        in_specs=[pl.BlockSpec((tm, D_MODEL), lambda i, g: (i, 0)),
                  pl.BlockSpec((tm, LANES), lambda i, g: (i, 0)),
                  pl.BlockSpec((tm, D_MODEL), lambda i, g: (i, 0)),
                  pl.BlockSpec((None, EXPERTS_PER_GROUP, D_MODEL, D_EXPERT), lambda i, g: (g, 0, 0, 0)),
                  pl.BlockSpec((None, EXPERTS_PER_GROUP, D_MODEL, D_EXPERT), lambda i, g: (g, 0, 0, 0)),
                  pl.BlockSpec((None, EXPERTS_PER_GROUP * D_EXPERT, D_MODEL), lambda i, g: (g, 0, 0)),
                  pl.BlockSpec((1, D_MODEL), lambda i, g: (0, 0))],
        out_specs=pl.BlockSpec((tm, D_MODEL), lambda i, g: (i, 0)),
        out_shape=jax.ShapeDtypeStruct((n, D_MODEL), F32),
        scratch_shapes=[pltpu.VMEM((tm, tm), BF16),
                        pltpu.VMEM((MOE_CHUNK, EXPERTS_PER_GROUP * D_EXPERT), BF16)],
        compiler_params=pltpu.CompilerParams(
            dimension_semantics=("parallel", "arbitrary"), vmem_limit_bytes=VMEM_LIMIT),
        name="moe",
    )(h3, comb, x2, wg, wu, wd, gf)


def _alibi_slopes(n_heads):
    return jnp.asarray([2.0 ** (-8.0 * (i + 1) / n_heads) for i in range(n_heads)], dtype=F32)


def _split_bf16(w):
    hi = w.astype(BF16)
    return hi, (w - hi.astype(F32)).astype(BF16)


def kernel(x, mem, mix_norm_g, w_in, moba_out_w, swa_out_w, swa_sinks, mix_out_w, xattn_norm_g, mem_norm_g, xattn_wq, xattn_wkv, xattn_wo, ffn_norm_g, router_group_w, router_group_b, router_expert_w, router_expert_b, expert_w_gate, expert_w_up, expert_w_down, final_norm_g):
    bsz, t, d = x.shape
    n = bsz * t
    assert w_in.shape[0] == 1, "the final RMSNorm is fused into the (single) layer's MoE kernel"
    l = 0
    x2d = x.reshape(n, d)
    mem2d = mem.reshape(-1, d)
    mem_len = mem.shape[1]
    moba_slopes = _alibi_slopes(MOBA_HEADS)
    swa_slopes = _alibi_slopes(SWA_Q_HEADS)

    w = w_in[l]
    c = 3 * MOBA_HEADS * HEAD_DIM + SWA_Q_HEADS * HEAD_DIM
    kvw = SWA_KV_HEADS * HEAD_DIM
    dup = lambda m: jnp.concatenate(
        [m[:, j * HEAD_DIM:(j + 1) * HEAD_DIM] for j in range(SWA_KV_HEADS) for _ in range(2)], axis=1)
    w_all = jnp.concatenate(
        [w[:, :c], dup(w[:, c:c + kvw]), dup(w[:, c + kvw:c + 2 * kvw]), w[:, c + 2 * kvw:]],
        axis=1).astype(BF16)
    qa, ka, va, qb, kd, vd, ga, gb = _in_proj(x2d, mix_norm_g[l].reshape(1, d), w_all)

    r3 = lambda arr: arr.reshape(bsz, t, arr.shape[1])
    a = _moba(moba_slopes, r3(qa), r3(ka), r3(va)).reshape(n, -1)
    b = _swa(swa_slopes, swa_sinks[l], r3(qb), r3(kd), r3(vd)).reshape(n, -1)

    kv = _mem_kv(mem2d, mem_norm_g[l].reshape(1, d), xattn_wkv[l].astype(BF16))
    kv3 = kv.reshape(bsz, mem_len, kv.shape[1])

    pad = LANES - _GROUP_LANE0 - N_GROUPS
    w_r = jnp.concatenate([router_expert_w[l], router_group_w[l], jnp.zeros((d, pad), F32)], axis=1)
    b_r = jnp.concatenate([router_expert_b[l], router_group_b[l], jnp.zeros((pad,), F32)]).reshape(1, LANES)
    wrh, wrl = _split_bf16(w_r)
    x2, h3, comb = _mix(
        x2d, a, b, ga, gb, kv3,
        moba_out_w[l].astype(BF16), swa_out_w[l].astype(BF16), mix_out_w[l].astype(BF16),
        xattn_norm_g[l].reshape(1, d), xattn_wq[l].astype(BF16), xattn_wo[l].astype(BF16),
        ffn_norm_g[l].reshape(1, d), wrh, wrl, b_r, t)

    wd = expert_w_down[l].astype(BF16).reshape(N_GROUPS, EXPERTS_PER_GROUP * D_EXPERT, d)
    out = _moe(h3, comb, x2, expert_w_gate[l].astype(BF16), expert_w_up[l].astype(BF16), wd,
               final_norm_g.reshape(1, d))
    return out.reshape(bsz, t, d)
```

```python
import numpy as np

import jax
import jax.numpy as jnp
from jax import lax
from jax.experimental import pallas as pl
from jax.experimental.pallas import tpu as pltpu

F32 = jnp.float32
BF16 = jnp.bfloat16

D_MODEL = 1024
HEAD_DIM = 64
MOBA_HEADS = 8
MOBA_BLOCK = 256
MOBA_TOPK = 3
SWA_Q_HEADS = 8
SWA_KV_HEADS = 2
SWA_WINDOW = 128
XATTN_HEADS = 4
XATTN_HEAD_DIM = 128
N_GROUPS = 4
EXPERTS_PER_GROUP = 8
D_EXPERT = 256
RMS_EPS = 1e-6
NEG = -1e30
LANES = 128
VMEM_LIMIT = 56 * 1024 * 1024

IN_TILE = 512
MIX_TILE = 512
MIX_SLAB = 256
MOE_TILE = 1024
MOE_CHUNK = LANES
MEM_TILE = 512


def _rms(x, g):
    return x * lax.rsqrt(jnp.mean(x * x, axis=-1, keepdims=True) + RMS_EPS) * g


def _dot(a, b):
    return jnp.dot(a, b, preferred_element_type=F32)


def _dot_nt(a, b):
    return lax.dot_general(a, b, (((1,), (1,)), ((), ())), preferred_element_type=F32)


_IN_WIDTHS = (512, 512, 512, 512, 256, 256, 1024, 1024)


def _in_proj_kernel(x_ref, g_ref, w_ref, *out_refs):
    h = _rms(x_ref[...], g_ref[...]).astype(BF16)
    c0 = 0
    for o_ref, width in zip(out_refs, _IN_WIDTHS):
        o_ref[...] = _dot(h, w_ref[:, c0:c0 + width]).astype(o_ref.dtype)
        c0 += width


def _in_proj(x2d, g, w_all):
    n = x2d.shape[0]
    tm = IN_TILE
    wtot = w_all.shape[1]
    return pl.pallas_call(
        _in_proj_kernel,
        grid=(n // tm,),
        in_specs=[
            pl.BlockSpec((tm, D_MODEL), lambda i: (i, 0)),
            pl.BlockSpec((1, D_MODEL), lambda i: (0, 0)),
            pl.BlockSpec((D_MODEL, wtot), lambda i: (0, 0)),
        ],
        out_specs=[pl.BlockSpec((tm, w), lambda i: (i, 0)) for w in _IN_WIDTHS],
        out_shape=[jax.ShapeDtypeStruct((n, w), BF16) for w in _IN_WIDTHS],
        compiler_params=pltpu.CompilerParams(
            dimension_semantics=("parallel",), vmem_limit_bytes=VMEM_LIMIT),
        name="in_proj",
    )(x2d, g, w_all)


_OFF_LANE = 8


def _moba_constants(t_len):
    nb = t_len // MOBA_BLOCK
    r = np.arange(t_len)
    kfeat = np.zeros((t_len, LANES), np.float32)
    kfeat[r, r // MOBA_BLOCK] = 1.0
    kfeat[:, _OFF_LANE] = r % MOBA_BLOCK
    c = np.arange(nb * nb)
    pairs = np.zeros((2 * LANES, t_len), np.float32)
    pairs[c[:, None], r[None, :]] = ((c % nb)[:, None] == (r // MOBA_BLOCK)[None, :])
    pairs[LANES + c[:, None], r[None, :]] = ((c // nb)[:, None] == (r // MOBA_BLOCK)[None, :])
    place = np.zeros((nb, LANES, LANES), np.float32)
    for i in range(nb):
        for m in range(i):
            for n in range(nb):
                if m != n:
                    place[i, nb * m + n, n] = 1.0
    return (jnp.asarray(kfeat, BF16), jnp.asarray(pairs, BF16), jnp.asarray(place, BF16))


def _moba_kernel(slopes_ref, q_ref, k_ref, v_ref, kfeat_ref, pairs_ref, place_ref, o_ref, kaug_ref):
    hp = pl.program_id(1)
    t_len = q_ref.shape[0]
    nb = t_len // MOBA_BLOCK
    bs = MOBA_BLOCK

    k = k_ref[...]
    kaug_ref[:, 0:LANES] = k
    kaug_ref[:, LANES:2 * LANES] = kfeat_ref[...]
    kmean_t = _dot(pairs_ref[...], k) * (1.0 / bs)
    km = jnp.concatenate([kmean_t[0:LANES].T, kmean_t[LANES:2 * LANES].T], axis=1).astype(BF16)

    lane = lax.broadcasted_iota(jnp.int32, (bs, LANES), 1)
    lane2 = lax.broadcasted_iota(jnp.int32, (2 * bs, LANES), 1)
    rival_first = (lane2 // nb) < (lane2 % nb)
    s_row = lax.broadcasted_iota(jnp.int32, (2 * bs, bs), 0)
    s_col = lax.broadcasted_iota(jnp.int32, (2 * bs, bs), 1)
    causal = s_col <= (s_row % bs)
    slopes = (slopes_ref[2 * hp], slopes_ref[2 * hp + 1])
    scale = jnp.asarray(HEAD_DIM ** -0.5, BF16)

    for i in range(nb):
        q = q_ref[i * bs:(i + 1) * bs, :]
        zero = jnp.zeros_like(q)
        qs = jnp.concatenate([jnp.where(lane < 64, q, zero) * scale,
                              jnp.where(lane >= 64, q, zero) * scale], axis=0)
        gates = _dot(qs, km)
        g_n, g_m = gates[:, 0:LANES], gates[:, LANES:2 * LANES]
        beats = (g_m > g_n) | (rival_first & (g_m == g_n))
        rank = _dot(jnp.where(beats, 1.0, 0.0).astype(BF16), place_ref[i])
        chosen = (rank < MOBA_TOPK - 0.5) & (lane2 < i)
        feats = []
        for hh in range(2):
            picked = (slopes[hh] * bs) * (lane - i).astype(F32)
            other = jnp.where(lane == _OFF_LANE, slopes[hh], jnp.where(lane < i, NEG, 0.0))
            feats.append(jnp.where(chosen[hh * bs:(hh + 1) * bs], picked, other))
        q_aug = jnp.concatenate([qs, jnp.concatenate(feats, axis=0).astype(BF16)], axis=1)

        m_i = l_i = acc = None
        for j in [i] + list(range(i)):
            s = _dot_nt(q_aug, kaug_ref[j * bs:(j + 1) * bs, :])
            v_j = v_ref[j * bs:(j + 1) * bs, :]
            if j == i:
                s = jnp.where(causal, s, NEG)
                m_i = jnp.max(s, axis=1, keepdims=True)
                p = jnp.exp(s - m_i)
                l_i = jnp.sum(p, axis=1, keepdims=True)
                acc = _dot(p.astype(BF16), v_j)
            else:
                m_n = jnp.maximum(m_i, jnp.max(s, axis=1, keepdims=True))
                alpha = jnp.exp(m_i - m_n)
                p = jnp.exp(s - m_n)
                l_i = alpha * l_i + jnp.sum(p, axis=1, keepdims=True)
                acc = alpha * acc + _dot(p.astype(BF16), v_j)
                m_i = m_n
        o = acc / l_i
        o_ref[i * bs:(i + 1) * bs, :] = jnp.where(lane < 64, o[0:bs], o[bs:2 * bs]).astype(o_ref.dtype)


def _moba(slopes, q3, k3, v3):
    b, t, w = q3.shape
    kfeat, pairs, place = _moba_constants(t)
    spec = pl.BlockSpec((None, t, LANES), lambda bi, hp: (bi, 0, hp))
    const = lambda arr: pl.BlockSpec(arr.shape, lambda bi, hp: (0,) * arr.ndim)
    return pl.pallas_call(
        _moba_kernel,
        grid=(b, w // LANES),
        in_specs=[pl.BlockSpec(memory_space=pltpu.SMEM), spec, spec, spec,
                  const(kfeat), const(pairs), const(place)],
        out_specs=spec,
        out_shape=jax.ShapeDtypeStruct((b, t, w), BF16),
        scratch_shapes=[pltpu.VMEM((t, 2 * LANES), BF16)],
        compiler_params=pltpu.CompilerParams(
            dimension_semantics=("parallel", "arbitrary"), vmem_limit_bytes=VMEM_LIMIT),
        name="moba",
    )(slopes, q3, k3, v3, kfeat, pairs, place)


def _swa_kernel(slopes_ref, sinks_ref, q_ref, kd_ref, vd_ref, o_ref, kp_ref, vp_ref, bias_ref):
    w = SWA_WINDOW
    t_len = q_ref.shape[0]
    kp_ref[0:w, :] = jnp.zeros((w, kp_ref.shape[1]), BF16)
    vp_ref[0:w, :] = jnp.zeros((w, vp_ref.shape[1]), BF16)
    kp_ref[w:, :] = kd_ref[...]
    vp_ref[w:, :] = vd_ref[...]

    r = lax.broadcasted_iota(jnp.int32, (w, 2 * w), 0)
    c = lax.broadcasted_iota(jnp.int32, (w, 2 * w), 1)
    rel = w + r - c
    in_window = (rel >= 0) & (rel < w)
    rel_f = rel.astype(F32)
    for h in range(SWA_Q_HEADS):
        bias_ref[h] = jnp.where(in_window, -slopes_ref[h] * rel_f, NEG)
    lane_q = lax.broadcasted_iota(jnp.int32, (w, LANES), 1)
    scale = jnp.asarray(HEAD_DIM ** -0.5, BF16)

    for n in range(t_len // w):
        qs = n * w
        for slab in range(SWA_Q_HEADS // 2):
            kvh = slab // (SWA_Q_HEADS // SWA_KV_HEADS // 2)
            q = q_ref[qs:qs + w, slab * LANES:(slab + 1) * LANES]
            kk = kp_ref[qs:qs + 2 * w, kvh * LANES:(kvh + 1) * LANES]
            vv = vp_ref[qs:qs + 2 * w, kvh * LANES:(kvh + 1) * LANES]
            outs = []
            for half in range(2):
                h = 2 * slab + half
                head_lane = (lane_q >= 64 * half) & (lane_q < 64 * half + 64)
                qh = jnp.where(head_lane, q, jnp.zeros_like(q)) * scale
                s = _dot_nt(qh, kk) + bias_ref[h]
                if n == 0:
                    s = jnp.where(c >= w, s, NEG)
                sink = sinks_ref[h]
                m = jnp.maximum(jnp.max(s, axis=1, keepdims=True), sink)
                p = jnp.exp(s - m)
                l = jnp.sum(p, axis=1, keepdims=True) + jnp.exp(sink - m)
                outs.append(_dot(p.astype(BF16), vv) / l)
            o_ref[qs:qs + w, slab * LANES:(slab + 1) * LANES] = (
                jnp.where(lane_q < 64, outs[0], outs[1]).astype(o_ref.dtype))


def _swa(slopes, sinks, q3, kd3, vd3):
    b, t, wq = q3.shape
    wkv = kd3.shape[2]
    smem = pl.BlockSpec(memory_space=pltpu.SMEM)
    return pl.pallas_call(
        _swa_kernel,
        grid=(b,),
        in_specs=[smem, smem,
                  pl.BlockSpec((None, t, wq), lambda bi: (bi, 0, 0)),
                  pl.BlockSpec((None, t, wkv), lambda bi: (bi, 0, 0)),
                  pl.BlockSpec((None, t, wkv), lambda bi: (bi, 0, 0))],
        out_specs=pl.BlockSpec((None, t, wq), lambda bi: (bi, 0, 0)),
        out_shape=jax.ShapeDtypeStruct((b, t, wq), BF16),
        scratch_shapes=[pltpu.VMEM((t + SWA_WINDOW, wkv), BF16),
                        pltpu.VMEM((t + SWA_WINDOW, wkv), BF16),
                        pltpu.VMEM((SWA_Q_HEADS, SWA_WINDOW, 2 * SWA_WINDOW), F32)],
        compiler_params=pltpu.CompilerParams(
            dimension_semantics=("parallel",), vmem_limit_bytes=VMEM_LIMIT),
        name="swa",
    )(slopes, sinks, q3, kd3, vd3)


def _mem_kv_kernel(m_ref, g_ref, w_ref, o_ref):
    o_ref[...] = _dot(_rms(m_ref[...], g_ref[...]).astype(BF16), w_ref[...]).astype(o_ref.dtype)


def _mem_kv(mem2d, g, w_kv):
    n = mem2d.shape[0]
    tm = MEM_TILE
    wout = w_kv.shape[1]
    return pl.pallas_call(
        _mem_kv_kernel,
        grid=(n // tm,),
        in_specs=[pl.BlockSpec((tm, D_MODEL), lambda i: (i, 0)),
                  pl.BlockSpec((1, D_MODEL), lambda i: (0, 0)),
                  pl.BlockSpec((D_MODEL, wout), lambda i: (0, 0))],
        out_specs=pl.BlockSpec((tm, wout), lambda i: (i, 0)),
        out_shape=jax.ShapeDtypeStruct((n, wout), BF16),
        compiler_params=pltpu.CompilerParams(
            dimension_semantics=("parallel",), vmem_limit_bytes=VMEM_LIMIT),
        name="mem_kv",
    )(mem2d, g, w_kv)


_GROUP_LANE0 = N_GROUPS * EXPERTS_PER_GROUP
_LOW = -3.0e38


def _route(logits):
    lane = lax.broadcasted_iota(jnp.int32, logits.shape, 1)
    lane_f = lane.astype(F32)
    is_g = (lane >= _GROUP_LANE0) & (lane < _GROUP_LANE0 + N_GROUPS)
    gl = jnp.where(is_g, logits, _LOW)
    gmax = jnp.max(gl, axis=1, keepdims=True)
    g_p = 1.0 / jnp.sum(jnp.exp(gl - gmax), axis=1, keepdims=True)
    g_idx = jnp.min(jnp.where(gl == gmax, lane_f - _GROUP_LANE0, 1e9), axis=1, keepdims=True)
    in_group = (lane < _GROUP_LANE0) & ((lane // EXPERTS_PER_GROUP).astype(F32) == g_idx)
    el = jnp.where(in_group, logits, _LOW)
    m1 = jnp.max(el, axis=1, keepdims=True)
    i1 = jnp.min(jnp.where(el == m1, lane_f, 1e9), axis=1, keepdims=True)
    el2 = jnp.where(lane_f == i1, _LOW, el)
    m2 = jnp.max(el2, axis=1, keepdims=True)
    i2 = jnp.min(jnp.where(el2 == m2, lane_f, 1e9), axis=1, keepdims=True)
    t = jnp.exp(m2 - m1)
    w1 = g_p / (1.0 + t)
    w2 = w1 * t
    return jnp.where(lane_f == i1, w1, jnp.where(lane_f == i2, w2, 0.0))


def _mix_kernel(x_ref, a_ref, b_ref, ga_ref, gb_ref, kv_ref,
                wa_ref, wb_ref, wo_ref, g2_ref, wq_ref, wxo_ref, g3_ref,
                wrh_ref, wrl_ref, br_ref,
                x2_ref, h3_ref, comb_ref):
    for r0 in range(0, x_ref.shape[0], MIX_SLAB):
        rows = slice(r0, r0 + MIX_SLAB)
        _mix_rows(rows, x_ref, a_ref, b_ref, ga_ref, gb_ref, kv_ref,
                  wa_ref, wb_ref, wo_ref, g2_ref, wq_ref, wxo_ref, g3_ref,
                  wrh_ref, wrl_ref, br_ref, x2_ref, h3_ref, comb_ref)


def _mix_rows(rows, x_ref, a_ref, b_ref, ga_ref, gb_ref, kv_ref,
              wa_ref, wb_ref, wo_ref, g2_ref, wq_ref, wxo_ref, g3_ref,
              wrh_ref, wrl_ref, br_ref, x2_ref, h3_ref, comb_ref):
    pa = _dot(a_ref[rows, :], wa_ref[...])
    pb = _dot(b_ref[rows, :], wb_ref[...])
    merged = (jax.nn.sigmoid(ga_ref[rows, :].astype(F32)) * pa
              + jax.nn.sigmoid(gb_ref[rows, :].astype(F32)) * pb)
    x1 = x_ref[rows, :] + _dot(merged.astype(BF16), wo_ref[...])

    q = _dot(_rms(x1, g2_ref[...]).astype(BF16), wq_ref[...]).astype(BF16)
    width = XATTN_HEADS * XATTN_HEAD_DIM
    heads = []
    for hd in range(XATTN_HEADS):
        lo = hd * XATTN_HEAD_DIM
        kx = kv_ref[:, lo:lo + XATTN_HEAD_DIM]
        vx = kv_ref[:, width + lo:width + lo + XATTN_HEAD_DIM]
        s = _dot_nt(q[:, lo:lo + XATTN_HEAD_DIM], kx) * (XATTN_HEAD_DIM ** -0.5)
        m = jnp.max(s, axis=1, keepdims=True)
        p = jnp.exp(s - m)
        l = jnp.sum(p, axis=1, keepdims=True)
        heads.append((_dot(p.astype(BF16), vx) / l).astype(BF16))
    x2 = x1 + _dot(jnp.concatenate(heads, axis=1), wxo_ref[...])
    x2_ref[rows, :] = x2

    h3 = _rms(x2, g3_ref[...])
    h3_hi = h3.astype(BF16)
    h3_lo = (h3 - h3_hi.astype(F32)).astype(BF16)
    h3_ref[rows, :] = h3_hi
    hi_terms = _dot(h3_hi, jnp.concatenate([wrh_ref[...], wrl_ref[...]], axis=1))
    logits = (hi_terms[:, 0:LANES]
              + (hi_terms[:, LANES:2 * LANES] + _dot(h3_lo, wrh_ref[...]))
              + br_ref[...])
    comb_ref[rows, :] = _route(logits)


def _mix(x2d, a, b, ga, gb, kv3, wa, wb, wo, g2, wq, wxo, g3, wrh, wrl, br, seq_len):
    n = x2d.shape[0]
    tm = MIX_TILE
    tiles_per_seq = seq_len // tm
    row = lambda w: pl.BlockSpec((tm, w), lambda i: (i, 0))
    full = lambda arr: pl.BlockSpec(arr.shape, lambda i: (0,) * arr.ndim)
    mem_len, kvw = kv3.shape[1], kv3.shape[2]
    return pl.pallas_call(
        _mix_kernel,
        grid=(n // tm,),
        in_specs=[row(D_MODEL), row(a.shape[1]), row(b.shape[1]), row(D_MODEL), row(D_MODEL),
                  pl.BlockSpec((None, mem_len, kvw), lambda i: (i // tiles_per_seq, 0, 0)),
                  full(wa), full(wb), full(wo), full(g2), full(wq), full(wxo), full(g3),
                  full(wrh), full(wrl), full(br)],
        out_specs=[row(D_MODEL), row(D_MODEL), row(LANES)],
        out_shape=[jax.ShapeDtypeStruct((n, D_MODEL), F32),
                   jax.ShapeDtypeStruct((n, D_MODEL), BF16),
                   jax.ShapeDtypeStruct((n, LANES), F32)],
        compiler_params=pltpu.CompilerParams(
            dimension_semantics=("parallel",), vmem_limit_bytes=VMEM_LIMIT),
        name="mix",
    )(x2d, a, b, ga, gb, kv3, wa, wb, wo, g2, wq, wxo, g3, wrh, wrl, br)


def _dot_tn(a, b):
    return lax.dot_general(a, b, (((0,), (0,)), ((), ())), preferred_element_type=F32)


def _moe_kernel(h_ref, comb_ref, x_ref, wg_ref, wu_ref, wd_ref, gf_ref, o_ref,
                slot_ref, terms_ref, cnt_ref, hid_ref):
    g = pl.program_id(1)
    tm = h_ref.shape[0]
    ch = MOE_CHUNK

    lane = lax.broadcasted_iota(jnp.int32, (tm, LANES), 1)
    lane_f = lane.astype(F32)
    lo = g * EXPERTS_PER_GROUP
    span = LANES // N_GROUPS

    @pl.when(g == 0)
    def _():
        o_ref[...] = jnp.zeros(o_ref.shape, o_ref.dtype)
        r = lax.broadcasted_iota(jnp.int32, (tm, tm), 0)
        c = lax.broadcasted_iota(jnp.int32, (tm, tm), 1)
        tri = jnp.where(c < r, 1.0, 0.0).astype(BF16)
        comb = comb_ref[...]
        rr = lax.broadcasted_iota(jnp.int32, (LANES, LANES), 0)
        cc = lax.broadcasted_iota(jnp.int32, (LANES, LANES), 1)
        to_group = jnp.where((rr < _GROUP_LANE0) & (rr // EXPERTS_PER_GROUP == cc // span), 1.0, 0.0)
        routed = jnp.where((comb > 0.0) & (lane < _GROUP_LANE0), 1.0, 0.0).astype(BF16)
        member = _dot(routed, to_group.astype(BF16)) > 0.5
        rank = _dot(tri, jnp.where(member, 1.0, 0.0).astype(BF16))
        slot = jnp.where(member, rank + 1.0, 0.0)
        hi = jnp.floor(slot * (1.0 / 32.0))
        slot_ref[...] = jnp.concatenate([hi, slot - 32.0 * hi], axis=1).astype(BF16)
        for q in range(N_GROUPS):
            cnt_ref[q] = jnp.max(slot[:, q * span:(q + 1) * span]).astype(jnp.int32)
        c1 = comb.astype(BF16)
        r1 = comb - c1.astype(F32)
        c2 = r1.astype(BF16)
        c3 = (r1 - c2.astype(F32)).astype(BF16)
        terms_ref[...] = jnp.concatenate([c1, c2, c3], axis=1)

    kk = lax.broadcasted_iota(jnp.int32, (2 * LANES, LANES), 0)
    pick = jnp.where(kk == g * span, 32.0, jnp.where(kk == LANES + g * span, 1.0, 0.0)).astype(BF16)
    slot_g = _dot(slot_ref[...], pick)
    n_chunks = (cnt_ref[g] + (ch - 1)) // ch

    def chunk(ci, carry):
        base = (ci * ch + 1).astype(F32)
        onehot = jnp.where(slot_g == base + lane_f, 1.0, 0.0).astype(BF16)
        hc = _dot_tn(onehot, h_ref[...]).astype(BF16)
        ct = _dot_tn(onehot, terms_ref[...])
        cc = (ct[:, 0:LANES] + ct[:, LANES:2 * LANES]) + ct[:, 2 * LANES:3 * LANES]
        lane_c = lax.broadcasted_iota(jnp.int32, cc.shape, 1)
        for e in range(EXPERTS_PER_GROUP):
            gate = _dot(hc, wg_ref[e])
            up = _dot(hc, wu_ref[e])
            c = jnp.sum(jnp.where(lane_c == lo + e, cc, 0.0), axis=1, keepdims=True)
            hid_ref[:, e * D_EXPERT:(e + 1) * D_EXPERT] = (
                gate * jax.nn.sigmoid(gate) * up * c).astype(BF16)
        y = _dot(hid_ref[...], wd_ref[...])
        y_hi = y.astype(BF16)
        y_lo = (y - y_hi.astype(F32)).astype(BF16)
        o_ref[...] += _dot(jnp.concatenate([onehot, onehot], axis=1),
                           jnp.concatenate([y_hi, y_lo], axis=0))
        return carry

    lax.fori_loop(0, n_chunks, chunk, 0)

    @pl.when(g == N_GROUPS - 1)
    def _():
        o_ref[...] = _rms(x_ref[...] + o_ref[...], gf_ref[...])


def _moe(h3, comb, x2, wg, wu, wd, gf):
    n = h3.shape[0]
    tm = MOE_TILE
    return pl.pallas_call(
        _moe_kernel,
        grid=(n // tm, N_GROUPS),
        in_specs=[pl.BlockSpec((tm, D_MODEL), lambda i, g: (i, 0)),
                  pl.BlockSpec((tm, LANES), lambda i, g: (i, 0)),
                  pl.BlockSpec((tm, D_MODEL), lambda i, g: (i, 0)),
                  pl.BlockSpec((None, EXPERTS_PER_GROUP, D_MODEL, D_EXPERT), lambda i, g: (g, 0, 0, 0)),
                  pl.BlockSpec((None, EXPERTS_PER_GROUP, D_MODEL, D_EXPERT), lambda i, g: (g, 0, 0, 0)),
                  pl.BlockSpec((None, EXPERTS_PER_GROUP * D_EXPERT, D_MODEL), lambda i, g: (g, 0, 0)),
                  pl.BlockSpec((1, D_MODEL), lambda i, g: (0, 0))],
        out_specs=pl.BlockSpec((tm, D_MODEL), lambda i, g: (i, 0)),
        out_shape=jax.ShapeDtypeStruct((n, D_MODEL), F32),
        scratch_shapes=[pltpu.VMEM((tm, 2 * LANES), BF16),
                        pltpu.VMEM((tm, 3 * LANES), BF16),
                        pltpu.SMEM((N_GROUPS,), jnp.int32),
                        pltpu.VMEM((MOE_CHUNK, EXPERTS_PER_GROUP * D_EXPERT), BF16)],
        compiler_params=pltpu.CompilerParams(
            dimension_semantics=("parallel", "arbitrary"), vmem_limit_bytes=VMEM_LIMIT),
        name="moe",
    )(h3, comb, x2, wg, wu, wd, gf)


def _alibi_slopes(n_heads):
    return jnp.asarray([2.0 ** (-8.0 * (i + 1) / n_heads) for i in range(n_heads)], dtype=F32)


def _split_bf16(w):
    hi = w.astype(BF16)
    return hi, (w - hi.astype(F32)).astype(BF16)


def kernel(x, mem, mix_norm_g, w_in, moba_out_w, swa_out_w, swa_sinks, mix_out_w, xattn_norm_g, mem_norm_g, xattn_wq, xattn_wkv, xattn_wo, ffn_norm_g, router_group_w, router_group_b, router_expert_w, router_expert_b, expert_w_gate, expert_w_up, expert_w_down, final_norm_g):
    bsz, t, d = x.shape
    n = bsz * t
    assert w_in.shape[0] == 1, "the final RMSNorm is fused into the (single) layer's MoE kernel"
    l = 0
    x2d = x.reshape(n, d)
    mem2d = mem.reshape(-1, d)
    mem_len = mem.shape[1]
    moba_slopes = _alibi_slopes(MOBA_HEADS)
    swa_slopes = _alibi_slopes(SWA_Q_HEADS)

    w = w_in[l]
    c = 3 * MOBA_HEADS * HEAD_DIM + SWA_Q_HEADS * HEAD_DIM
    kvw = SWA_KV_HEADS * HEAD_DIM
    dup = lambda m: jnp.concatenate(
        [m[:, j * HEAD_DIM:(j + 1) * HEAD_DIM] for j in range(SWA_KV_HEADS) for _ in range(2)], axis=1)
    w_all = jnp.concatenate(
        [w[:, :c], dup(w[:, c:c + kvw]), dup(w[:, c + kvw:c + 2 * kvw]), w[:, c + 2 * kvw:]],
        axis=1).astype(BF16)
    qa, ka, va, qb, kd, vd, ga, gb = _in_proj(x2d, mix_norm_g[l].reshape(1, d), w_all)

    r3 = lambda arr: arr.reshape(bsz, t, arr.shape[1])
    a = _moba(moba_slopes, r3(qa), r3(ka), r3(va)).reshape(n, -1)
    b = _swa(swa_slopes, swa_sinks[l], r3(qb), r3(kd), r3(vd)).reshape(n, -1)

    kv = _mem_kv(mem2d, mem_norm_g[l].reshape(1, d), xattn_wkv[l].astype(BF16))
    kv3 = kv.reshape(bsz, mem_len, kv.shape[1])

    pad = LANES - _GROUP_LANE0 - N_GROUPS
    w_r = jnp.concatenate([router_expert_w[l], router_group_w[l], jnp.zeros((d, pad), F32)], axis=1)
    b_r = jnp.concatenate([router_expert_b[l], router_group_b[l], jnp.zeros((pad,), F32)]).reshape(1, LANES)
    wrh, wrl = _split_bf16(w_r)
    x2, h3, comb = _mix(
        x2d, a, b, ga, gb, kv3,
        moba_out_w[l].astype(BF16), swa_out_w[l].astype(BF16), mix_out_w[l].astype(BF16),
        xattn_norm_g[l].reshape(1, d), xattn_wq[l].astype(BF16), xattn_wo[l].astype(BF16),
        ffn_norm_g[l].reshape(1, d), wrh, wrl, b_r, t)

    wd = expert_w_down[l].astype(BF16).reshape(N_GROUPS, EXPERTS_PER_GROUP * D_EXPERT, d)
    out = _moe(h3, comb, x2, expert_w_gate[l].astype(BF16), expert_w_up[l].astype(BF16), wd,
               final_norm_g.reshape(1, d))
    return out.reshape(bsz, t, d)
```

```python
import numpy as np

import jax
import jax.numpy as jnp
from jax import lax
from jax.experimental import pallas as pl
from jax.experimental.pallas import tpu as pltpu

F32 = jnp.float32
BF16 = jnp.bfloat16

D_MODEL = 1024
HEAD_DIM = 64
MOBA_HEADS = 8
MOBA_BLOCK = 256
MOBA_TOPK = 3
SWA_Q_HEADS = 8
SWA_KV_HEADS = 2
SWA_WINDOW = 128
XATTN_HEADS = 4
XATTN_HEAD_DIM = 128
N_GROUPS = 4
EXPERTS_PER_GROUP = 8
D_EXPERT = 256
RMS_EPS = 1e-6
NEG = -1e30
LANES = 128
VMEM_LIMIT = 56 * 1024 * 1024

IN_TILE = 512
MIX_TILE = 512
MIX_SLAB = 256
MOE_TILE = 1024
MOE_CHUNK = LANES
MEM_TILE = 512


def _rms(x, g):
    return x * lax.rsqrt(jnp.mean(x * x, axis=-1, keepdims=True) + RMS_EPS) * g


def _dot(a, b):
    return jnp.dot(a, b, preferred_element_type=F32)


def _dot_nt(a, b):
    return lax.dot_general(a, b, (((1,), (1,)), ((), ())), preferred_element_type=F32)


_IN_WIDTHS = (512, 512, 512, 512, 256, 256, 1024, 1024)


def _in_proj_kernel(x_ref, g_ref, w_ref, *out_refs):
    h = _rms(x_ref[...], g_ref[...]).astype(BF16)
    c0 = 0
    for o_ref, width in zip(out_refs, _IN_WIDTHS):
        o_ref[...] = _dot(h, w_ref[:, c0:c0 + width]).astype(o_ref.dtype)
        c0 += width


def _in_proj(x2d, g, w_all):
    n = x2d.shape[0]
    tm = IN_TILE
    wtot = w_all.shape[1]
    return pl.pallas_call(
        _in_proj_kernel,
        grid=(n // tm,),
        in_specs=[
            pl.BlockSpec((tm, D_MODEL), lambda i: (i, 0)),
            pl.BlockSpec((1, D_MODEL), lambda i: (0, 0)),
            pl.BlockSpec((D_MODEL, wtot), lambda i: (0, 0)),
        ],
        out_specs=[pl.BlockSpec((tm, w), lambda i: (i, 0)) for w in _IN_WIDTHS],
        out_shape=[jax.ShapeDtypeStruct((n, w), BF16) for w in _IN_WIDTHS],
        compiler_params=pltpu.CompilerParams(
            dimension_semantics=("parallel",), vmem_limit_bytes=VMEM_LIMIT),
        name="in_proj",
    )(x2d, g, w_all)


_OFF_LANE = 8


def _moba_constants(t_len):
    nb = t_len // MOBA_BLOCK
    r = np.arange(t_len)
    kfeat = np.zeros((t_len, LANES), np.float32)
    kfeat[r, r // MOBA_BLOCK] = 1.0
    kfeat[:, _OFF_LANE] = r % MOBA_BLOCK
    c = np.arange(nb * nb)
    pairs = np.zeros((2 * LANES, t_len), np.float32)
    pairs[c[:, None], r[None, :]] = ((c % nb)[:, None] == (r // MOBA_BLOCK)[None, :])
    pairs[LANES + c[:, None], r[None, :]] = ((c // nb)[:, None] == (r // MOBA_BLOCK)[None, :])
    place = np.zeros((nb, LANES, LANES), np.float32)
    for i in range(nb):
        for m in range(i):
            for n in range(nb):
                if m != n:
                    place[i, nb * m + n, n] = 1.0
    return (jnp.asarray(kfeat, BF16), jnp.asarray(pairs, BF16), jnp.asarray(place, BF16))


def _moba_kernel(slopes_ref, q_ref, k_ref, v_ref, kfeat_ref, pairs_ref, place_ref, o_ref, kaug_ref):
    hp = pl.program_id(1)
    t_len = q_ref.shape[0]
    nb = t_len // MOBA_BLOCK
    bs = MOBA_BLOCK

    k = k_ref[...]
    kaug_ref[:, 0:LANES] = k
    kaug_ref[:, LANES:2 * LANES] = kfeat_ref[...]
    kmean_t = _dot(pairs_ref[...], k) * (1.0 / bs)
    km = jnp.concatenate([kmean_t[0:LANES].T, kmean_t[LANES:2 * LANES].T], axis=1).astype(BF16)

    lane = lax.broadcasted_iota(jnp.int32, (bs, LANES), 1)
    lane2 = lax.broadcasted_iota(jnp.int32, (2 * bs, LANES), 1)
    rival_first = (lane2 // nb) < (lane2 % nb)
    s_row = lax.broadcasted_iota(jnp.int32, (2 * bs, bs), 0)
    s_col = lax.broadcasted_iota(jnp.int32, (2 * bs, bs), 1)
    causal = s_col <= (s_row % bs)
    slopes = (slopes_ref[2 * hp], slopes_ref[2 * hp + 1])
    scale = jnp.asarray(HEAD_DIM ** -0.5, BF16)

    def q_block(i):
        q = q_ref[i * bs:(i + 1) * bs, :]
        zero = jnp.zeros_like(q)
        qs = jnp.concatenate([jnp.where(lane < 64, q, zero) * scale,
                              jnp.where(lane >= 64, q, zero) * scale], axis=0)
        gates = _dot(qs, km)
        g_n, g_m = gates[:, 0:LANES], gates[:, LANES:2 * LANES]
        beats = (g_m > g_n) | (rival_first & (g_m == g_n))
        rank = _dot(jnp.where(beats, 1.0, 0.0).astype(BF16), place_ref[i])
        chosen = (rank < MOBA_TOPK - 0.5) & (lane2 < i)
        feats = []
        for hh in range(2):
            picked = (slopes[hh] * bs) * (lane - i).astype(F32)
            other = jnp.where(lane == _OFF_LANE, slopes[hh], jnp.where(lane < i, NEG, 0.0))
            feats.append(jnp.where(chosen[hh * bs:(hh + 1) * bs], picked, other))
        q_aug = jnp.concatenate([qs, jnp.concatenate(feats, axis=0).astype(BF16)], axis=1)
        yield

        m_i = l_i = acc = None
        for j in [i] + list(range(i)):
            if j != i:
                yield
            s = _dot_nt(q_aug, kaug_ref[j * bs:(j + 1) * bs, :])
            v_j = v_ref[j * bs:(j + 1) * bs, :]
            if j == i:
                s = jnp.where(causal, s, NEG)
                m_i = jnp.max(s, axis=1, keepdims=True)
                p = jnp.exp(s - m_i)
                l_i = jnp.sum(p, axis=1, keepdims=True)
                acc = _dot(p.astype(BF16), v_j)
            else:
                m_n = jnp.maximum(m_i, jnp.max(s, axis=1, keepdims=True))
                alpha = jnp.exp(m_i - m_n)
                p = jnp.exp(s - m_n)
                l_i = alpha * l_i + jnp.sum(p, axis=1, keepdims=True)
                acc = alpha * acc + _dot(p.astype(BF16), v_j)
                m_i = m_n
        o = acc / l_i
        o_ref[i * bs:(i + 1) * bs, :] = jnp.where(lane < 64, o[0:bs], o[bs:2 * bs]).astype(o_ref.dtype)

    assert nb % 4 == 0
    for first in range(0, nb // 2, 2):
        live = [q_block(nb - 1 - first), q_block(nb - 2 - first), q_block(first + 1), q_block(first)]
        while live:
            for blk in list(live):
                if next(blk, "done") == "done":
                    live.remove(blk)


def _moba(slopes, q3, k3, v3):
    b, t, w = q3.shape
    kfeat, pairs, place = _moba_constants(t)
    spec = pl.BlockSpec((None, t, LANES), lambda bi, hp: (bi, 0, hp))
    const = lambda arr: pl.BlockSpec(arr.shape, lambda bi, hp: (0,) * arr.ndim)
    return pl.pallas_call(
        _moba_kernel,
        grid=(b, w // LANES),
        in_specs=[pl.BlockSpec(memory_space=pltpu.SMEM), spec, spec, spec,
                  const(kfeat), const(pairs), const(place)],
        out_specs=spec,
        out_shape=jax.ShapeDtypeStruct((b, t, w), BF16),
        scratch_shapes=[pltpu.VMEM((t, 2 * LANES), BF16)],
        compiler_params=pltpu.CompilerParams(
            dimension_semantics=("parallel", "arbitrary"), vmem_limit_bytes=VMEM_LIMIT),
        name="moba",
    )(slopes, q3, k3, v3, kfeat, pairs, place)


def _swa_kernel(slopes_ref, sinks_ref, q_ref, kd_ref, vd_ref, o_ref, kp_ref, vp_ref, bias_ref):
    w = SWA_WINDOW
    t_len = q_ref.shape[0]
    kp_ref[0:w, :] = jnp.zeros((w, kp_ref.shape[1]), BF16)
    vp_ref[0:w, :] = jnp.zeros((w, vp_ref.shape[1]), BF16)
    kp_ref[w:, :] = kd_ref[...]
    vp_ref[w:, :] = vd_ref[...]

    r = lax.broadcasted_iota(jnp.int32, (w, 2 * w), 0)
    c = lax.broadcasted_iota(jnp.int32, (w, 2 * w), 1)
    rel = w + r - c
    in_window = (rel >= 0) & (rel < w)
    rel_f = rel.astype(F32)
    for h in range(SWA_Q_HEADS):
        bias_ref[h] = jnp.where(in_window, -slopes_ref[h] * rel_f, NEG)
    lane_q = lax.broadcasted_iota(jnp.int32, (w, LANES), 1)
    scale = jnp.asarray(HEAD_DIM ** -0.5, BF16)

    for n in range(t_len // w):
        qs = n * w
        for slab in range(SWA_Q_HEADS // 2):
            kvh = slab // (SWA_Q_HEADS // SWA_KV_HEADS // 2)
            q = q_ref[qs:qs + w, slab * LANES:(slab + 1) * LANES]
            kk = kp_ref[qs:qs + 2 * w, kvh * LANES:(kvh + 1) * LANES]
            vv = vp_ref[qs:qs + 2 * w, kvh * LANES:(kvh + 1) * LANES]
            outs = []
            for half in range(2):
                h = 2 * slab + half
                head_lane = (lane_q >= 64 * half) & (lane_q < 64 * half + 64)
                qh = jnp.where(head_lane, q, jnp.zeros_like(q)) * scale
                s = _dot_nt(qh, kk) + bias_ref[h]
                if n == 0:
                    s = jnp.where(c >= w, s, NEG)
                sink = sinks_ref[h]
                m = jnp.maximum(jnp.max(s, axis=1, keepdims=True), sink)
                p = jnp.exp(s - m)
                l = jnp.sum(p, axis=1, keepdims=True) + jnp.exp(sink - m)
                outs.append(_dot(p.astype(BF16), vv) / l)
            o_ref[qs:qs + w, slab * LANES:(slab + 1) * LANES] = (
                jnp.where(lane_q < 64, outs[0], outs[1]).astype(o_ref.dtype))


def _swa(slopes, sinks, q3, kd3, vd3):
    b, t, wq = q3.shape
    wkv = kd3.shape[2]
    smem = pl.BlockSpec(memory_space=pltpu.SMEM)
    return pl.pallas_call(
        _swa_kernel,
        grid=(b,),
        in_specs=[smem, smem,
                  pl.BlockSpec((None, t, wq), lambda bi: (bi, 0, 0)),
                  pl.BlockSpec((None, t, wkv), lambda bi: (bi, 0, 0)),
                  pl.BlockSpec((None, t, wkv), lambda bi: (bi, 0, 0))],
        out_specs=pl.BlockSpec((None, t, wq), lambda bi: (bi, 0, 0)),
        out_shape=jax.ShapeDtypeStruct((b, t, wq), BF16),
        scratch_shapes=[pltpu.VMEM((t + SWA_WINDOW, wkv), BF16),
                        pltpu.VMEM((t + SWA_WINDOW, wkv), BF16),
                        pltpu.VMEM((SWA_Q_HEADS, SWA_WINDOW, 2 * SWA_WINDOW), F32)],
        compiler_params=pltpu.CompilerParams(
            dimension_semantics=("parallel",), vmem_limit_bytes=VMEM_LIMIT),
        name="swa",
    )(slopes, sinks, q3, kd3, vd3)


def _mem_kv_kernel(m_ref, g_ref, w_ref, o_ref):
    o_ref[...] = _dot(_rms(m_ref[...], g_ref[...]).astype(BF16), w_ref[...]).astype(o_ref.dtype)


def _mem_kv(mem2d, g, w_kv):
    n = mem2d.shape[0]
    tm = MEM_TILE
    wout = w_kv.shape[1]
    return pl.pallas_call(
        _mem_kv_kernel,
        grid=(n // tm,),
        in_specs=[pl.BlockSpec((tm, D_MODEL), lambda i: (i, 0)),
                  pl.BlockSpec((1, D_MODEL), lambda i: (0, 0)),
                  pl.BlockSpec((D_MODEL, wout), lambda i: (0, 0))],
        out_specs=pl.BlockSpec((tm, wout), lambda i: (i, 0)),
        out_shape=jax.ShapeDtypeStruct((n, wout), BF16),
        compiler_params=pltpu.CompilerParams(
            dimension_semantics=("parallel",), vmem_limit_bytes=VMEM_LIMIT),
        name="mem_kv",
    )(mem2d, g, w_kv)


_GROUP_LANE0 = N_GROUPS * EXPERTS_PER_GROUP
_LOW = -3.0e38


def _route(logits):
    lane = lax.broadcasted_iota(jnp.int32, logits.shape, 1)
    lane_f = lane.astype(F32)
    is_g = (lane >= _GROUP_LANE0) & (lane < _GROUP_LANE0 + N_GROUPS)
    gl = jnp.where(is_g, logits, _LOW)
    gmax = jnp.max(gl, axis=1, keepdims=True)
    g_p = 1.0 / jnp.sum(jnp.exp(gl - gmax), axis=1, keepdims=True)
    g_idx = jnp.min(jnp.where(gl == gmax, lane_f - _GROUP_LANE0, 1e9), axis=1, keepdims=True)
    in_group = (lane < _GROUP_LANE0) & ((lane // EXPERTS_PER_GROUP).astype(F32) == g_idx)
    el = jnp.where(in_group, logits, _LOW)
    m1 = jnp.max(el, axis=1, keepdims=True)
    i1 = jnp.min(jnp.where(el == m1, lane_f, 1e9), axis=1, keepdims=True)
    el2 = jnp.where(lane_f == i1, _LOW, el)
    m2 = jnp.max(el2, axis=1, keepdims=True)
    i2 = jnp.min(jnp.where(el2 == m2, lane_f, 1e9), axis=1, keepdims=True)
    t = jnp.exp(m2 - m1)
    w1 = g_p / (1.0 + t)
    w2 = w1 * t
    return jnp.where(lane_f == i1, w1, jnp.where(lane_f == i2, w2, 0.0))


def _mix_kernel(x_ref, a_ref, b_ref, ga_ref, gb_ref, kv_ref,
                wa_ref, wb_ref, wo_ref, g2_ref, wq_ref, wxo_ref, g3_ref,
                wrh_ref, wrl_ref, br_ref,
                x2_ref, h3_ref, comb_ref):
    slabs = [_mix_rows(slice(r0, r0 + MIX_SLAB), x_ref, a_ref, b_ref, ga_ref, gb_ref, kv_ref,
                       wa_ref, wb_ref, wo_ref, g2_ref, wq_ref, wxo_ref, g3_ref,
                       wrh_ref, wrl_ref, br_ref, x2_ref, h3_ref, comb_ref)
             for r0 in range(0, x_ref.shape[0], MIX_SLAB)]
    for _ in range(_MIX_STAGES):
        for slab in slabs:
            next(slab)


_MIX_STAGES = 8


def _mix_rows(rows, x_ref, a_ref, b_ref, ga_ref, gb_ref, kv_ref,
              wa_ref, wb_ref, wo_ref, g2_ref, wq_ref, wxo_ref, g3_ref,
              wrh_ref, wrl_ref, br_ref, x2_ref, h3_ref, comb_ref):
    pa = _dot(a_ref[rows, :], wa_ref[...])
    pb = _dot(b_ref[rows, :], wb_ref[...])
    yield
    merged = (jax.nn.sigmoid(ga_ref[rows, :].astype(F32)) * pa
              + jax.nn.sigmoid(gb_ref[rows, :].astype(F32)) * pb)
    x1 = x_ref[rows, :] + _dot(merged.astype(BF16), wo_ref[...])
    yield

    q = _dot(_rms(x1, g2_ref[...]).astype(BF16), wq_ref[...]).astype(BF16)
    yield
    width = XATTN_HEADS * XATTN_HEAD_DIM
    heads = []
    for hd in range(XATTN_HEADS):
        lo = hd * XATTN_HEAD_DIM
        kx = kv_ref[:, lo:lo + XATTN_HEAD_DIM]
        vx = kv_ref[:, width + lo:width + lo + XATTN_HEAD_DIM]
        s = _dot_nt(q[:, lo:lo + XATTN_HEAD_DIM], kx) * (XATTN_HEAD_DIM ** -0.5)
        m = jnp.max(s, axis=1, keepdims=True)
        p = jnp.exp(s - m)
        l = jnp.sum(p, axis=1, keepdims=True)
        heads.append((_dot(p.astype(BF16), vx) / l).astype(BF16))
        if hd % 2 == 1:
            yield
    x2 = x1 + _dot(jnp.concatenate(heads, axis=1), wxo_ref[...])
    x2_ref[rows, :] = x2
    yield

    h3 = _rms(x2, g3_ref[...])
    h3_hi = h3.astype(BF16)
    h3_lo = (h3 - h3_hi.astype(F32)).astype(BF16)
    h3_ref[rows, :] = h3_hi
    hi_terms = _dot(h3_hi, jnp.concatenate([wrh_ref[...], wrl_ref[...]], axis=1))
    logits = (hi_terms[:, 0:LANES]
              + (hi_terms[:, LANES:2 * LANES] + _dot(h3_lo, wrh_ref[...]))
              + br_ref[...])
    yield
    comb_ref[rows, :] = _route(logits)
    yield


def _mix(x2d, a, b, ga, gb, kv3, wa, wb, wo, g2, wq, wxo, g3, wrh, wrl, br, seq_len):
    n = x2d.shape[0]
    tm = MIX_TILE
    tiles_per_seq = seq_len // tm
    row = lambda w: pl.BlockSpec((tm, w), lambda i: (i, 0))
    full = lambda arr: pl.BlockSpec(arr.shape, lambda i: (0,) * arr.ndim)
    mem_len, kvw = kv3.shape[1], kv3.shape[2]
    return pl.pallas_call(
        _mix_kernel,
        grid=(n // tm,),
        in_specs=[row(D_MODEL), row(a.shape[1]), row(b.shape[1]), row(D_MODEL), row(D_MODEL),
                  pl.BlockSpec((None, mem_len, kvw), lambda i: (i // tiles_per_seq, 0, 0)),
                  full(wa), full(wb), full(wo), full(g2), full(wq), full(wxo), full(g3),
                  full(wrh), full(wrl), full(br)],
        out_specs=[row(D_MODEL), row(D_MODEL), row(LANES)],
        out_shape=[jax.ShapeDtypeStruct((n, D_MODEL), F32),
                   jax.ShapeDtypeStruct((n, D_MODEL), BF16),
                   jax.ShapeDtypeStruct((n, LANES), F32)],
        compiler_params=pltpu.CompilerParams(
            dimension_semantics=("parallel",), vmem_limit_bytes=VMEM_LIMIT),
        name="mix",
    )(x2d, a, b, ga, gb, kv3, wa, wb, wo, g2, wq, wxo, g3, wrh, wrl, br)


def _dot_tn(a, b):
    return lax.dot_general(a, b, (((0,), (0,)), ((), ())), preferred_element_type=F32)


def _moe_kernel(h_ref, comb_ref, x_ref, wg_ref, wu_ref, wd_ref, gf_ref, o_ref,
                slot_ref, terms_ref, cnt_ref, hid_ref):
    g = pl.program_id(1)
    tm = h_ref.shape[0]
    ch = MOE_CHUNK

    lane = lax.broadcasted_iota(jnp.int32, (tm, LANES), 1)
    lane_f = lane.astype(F32)
    lo = g * EXPERTS_PER_GROUP
    span = LANES // N_GROUPS

    @pl.when(g == 0)
    def _():
        o_ref[...] = jnp.zeros(o_ref.shape, o_ref.dtype)
        r = lax.broadcasted_iota(jnp.int32, (tm, tm), 0)
        c = lax.broadcasted_iota(jnp.int32, (tm, tm), 1)
        tri = jnp.where(c < r, 1.0, 0.0).astype(BF16)
        comb = comb_ref[...]
        rr = lax.broadcasted_iota(jnp.int32, (LANES, LANES), 0)
        cc = lax.broadcasted_iota(jnp.int32, (LANES, LANES), 1)
        to_group = jnp.where((rr < _GROUP_LANE0) & (rr // EXPERTS_PER_GROUP == cc // span), 1.0, 0.0)
        routed = jnp.where((comb > 0.0) & (lane < _GROUP_LANE0), 1.0, 0.0).astype(BF16)
        member = _dot(routed, to_group.astype(BF16)) > 0.5
        rank = _dot(tri, jnp.where(member, 1.0, 0.0).astype(BF16))
        slot = jnp.where(member, rank + 1.0, 0.0)
        hi = jnp.floor(slot * (1.0 / 32.0))
        slot_ref[...] = jnp.concatenate([hi, slot - 32.0 * hi], axis=1).astype(BF16)
        for q in range(N_GROUPS):
            cnt_ref[q] = jnp.max(slot[:, q * span:(q + 1) * span]).astype(jnp.int32)
        c1 = comb.astype(BF16)
        r1 = comb - c1.astype(F32)
        c2 = r1.astype(BF16)
        c3 = (r1 - c2.astype(F32)).astype(BF16)
        terms_ref[...] = jnp.concatenate([c1, c2, c3], axis=1)

    kk = lax.broadcasted_iota(jnp.int32, (2 * LANES, LANES), 0)
    pick = jnp.where(kk == g * span, 32.0, jnp.where(kk == LANES + g * span, 1.0, 0.0)).astype(BF16)
    slot_g = _dot(slot_ref[...], pick)
    n_chunks = (cnt_ref[g] + (ch - 1)) // ch

    def chunk(ci, carry):
        base = (ci * ch + 1).astype(F32)
        onehot = jnp.where(slot_g == base + lane_f, 1.0, 0.0).astype(BF16)
        hc = _dot_tn(onehot, h_ref[...]).astype(BF16)
        ct = _dot_tn(onehot, terms_ref[...])
        cc = (ct[:, 0:LANES] + ct[:, LANES:2 * LANES]) + ct[:, 2 * LANES:3 * LANES]
        lane_c = lax.broadcasted_iota(jnp.int32, cc.shape, 1)
        for e in range(EXPERTS_PER_GROUP):
            gate = _dot(hc, wg_ref[e])
            up = _dot(hc, wu_ref[e])
            c = jnp.sum(jnp.where(lane_c == lo + e, cc, 0.0), axis=1, keepdims=True)
            hid_ref[:, e * D_EXPERT:(e + 1) * D_EXPERT] = (
                gate * jax.nn.sigmoid(gate) * up * c).astype(BF16)
        y = _dot(hid_ref[...], wd_ref[...])
        y_hi = y.astype(BF16)
        y_lo = (y - y_hi.astype(F32)).astype(BF16)
        o_ref[...] += _dot(jnp.concatenate([onehot, onehot], axis=1),
                           jnp.concatenate([y_hi, y_lo], axis=0))
        return carry

    lax.fori_loop(0, n_chunks, chunk, 0)

    @pl.when(g == N_GROUPS - 1)
    def _():
        o_ref[...] = _rms(x_ref[...] + o_ref[...], gf_ref[...])


def _moe(h3, comb, x2, wg, wu, wd, gf):
    n = h3.shape[0]
    tm = MOE_TILE
    return pl.pallas_call(
        _moe_kernel,
        grid=(n // tm, N_GROUPS),
        in_specs=[pl.BlockSpec((tm, D_MODEL), lambda i, g: (i, 0)),
                  pl.BlockSpec((tm, LANES), lambda i, g: (i, 0)),
                  pl.BlockSpec((tm, D_MODEL), lambda i, g: (i, 0)),
                  pl.BlockSpec((None, EXPERTS_PER_GROUP, D_MODEL, D_EXPERT), lambda i, g: (g, 0, 0, 0)),
                  pl.BlockSpec((None, EXPERTS_PER_GROUP, D_MODEL, D_EXPERT), lambda i, g: (g, 0, 0, 0)),
                  pl.BlockSpec((None, EXPERTS_PER_GROUP * D_EXPERT, D_MODEL), lambda i, g: (g, 0, 0)),
                  pl.BlockSpec((1, D_MODEL), lambda i, g: (0, 0))],
        out_specs=pl.BlockSpec((tm, D_MODEL), lambda i, g: (i, 0)),
        out_shape=jax.ShapeDtypeStruct((n, D_MODEL), F32),
        scratch_shapes=[pltpu.VMEM((tm, 2 * LANES), BF16),
                        pltpu.VMEM((tm, 3 * LANES), BF16),
                        pltpu.SMEM((N_GROUPS,), jnp.int32),
                        pltpu.VMEM((MOE_CHUNK, EXPERTS_PER_GROUP * D_EXPERT), BF16)],
        compiler_params=pltpu.CompilerParams(
            dimension_semantics=("parallel", "arbitrary"), vmem_limit_bytes=VMEM_LIMIT),
        name="moe",
    )(h3, comb, x2, wg, wu, wd, gf)


def _alibi_slopes(n_heads):
    return jnp.asarray([2.0 ** (-8.0 * (i + 1) / n_heads) for i in range(n_heads)], dtype=F32)


def _split_bf16(w):
    hi = w.astype(BF16)
    return hi, (w - hi.astype(F32)).astype(BF16)


def kernel(x, mem, mix_norm_g, w_in, moba_out_w, swa_out_w, swa_sinks, mix_out_w, xattn_norm_g, mem_norm_g, xattn_wq, xattn_wkv, xattn_wo, ffn_norm_g, router_group_w, router_group_b, router_expert_w, router_expert_b, expert_w_gate, expert_w_up, expert_w_down, final_norm_g):
    bsz, t, d = x.shape
    n = bsz * t
    assert w_in.shape[0] == 1, "the final RMSNorm is fused into the (single) layer's MoE kernel"
    l = 0
    x2d = x.reshape(n, d)
    mem2d = mem.reshape(-1, d)
    mem_len = mem.shape[1]
    moba_slopes = _alibi_slopes(MOBA_HEADS)
    swa_slopes = _alibi_slopes(SWA_Q_HEADS)

    w = w_in[l]
    c = 3 * MOBA_HEADS * HEAD_DIM + SWA_Q_HEADS * HEAD_DIM
    kvw = SWA_KV_HEADS * HEAD_DIM
    dup = lambda m: jnp.concatenate(
        [m[:, j * HEAD_DIM:(j + 1) * HEAD_DIM] for j in range(SWA_KV_HEADS) for _ in range(2)], axis=1)
    w_all = jnp.concatenate(
        [w[:, :c], dup(w[:, c:c + kvw]), dup(w[:, c + kvw:c + 2 * kvw]), w[:, c + 2 * kvw:]],
        axis=1).astype(BF16)
    qa, ka, va, qb, kd, vd, ga, gb = _in_proj(x2d, mix_norm_g[l].reshape(1, d), w_all)

    r3 = lambda arr: arr.reshape(bsz, t, arr.shape[1])
    a = _moba(moba_slopes, r3(qa), r3(ka), r3(va)).reshape(n, -1)
    b = _swa(swa_slopes, swa_sinks[l], r3(qb), r3(kd), r3(vd)).reshape(n, -1)

    kv = _mem_kv(mem2d, mem_norm_g[l].reshape(1, d), xattn_wkv[l].astype(BF16))
    kv3 = kv.reshape(bsz, mem_len, kv.shape[1])

    pad = LANES - _GROUP_LANE0 - N_GROUPS
    w_r = jnp.concatenate([router_expert_w[l], router_group_w[l], jnp.zeros((d, pad), F32)], axis=1)
    b_r = jnp.concatenate([router_expert_b[l], router_group_b[l], jnp.zeros((pad,), F32)]).reshape(1, LANES)
    wrh, wrl = _split_bf16(w_r)
    x2, h3, comb = _mix(
        x2d, a, b, ga, gb, kv3,
        moba_out_w[l].astype(BF16), swa_out_w[l].astype(BF16), mix_out_w[l].astype(BF16),
        xattn_norm_g[l].reshape(1, d), xattn_wq[l].astype(BF16), xattn_wo[l].astype(BF16),
        ffn_norm_g[l].reshape(1, d), wrh, wrl, b_r, t)

    wd = expert_w_down[l].astype(BF16).reshape(N_GROUPS, EXPERTS_PER_GROUP * D_EXPERT, d)
    out = _moe(h3, comb, x2, expert_w_gate[l].astype(BF16), expert_w_up[l].astype(BF16), wd,
               final_norm_g.reshape(1, d))
    return out.reshape(bsz, t, d)
```

```python
import numpy as np

import jax
import jax.numpy as jnp
from jax import lax
from jax.experimental import pallas as pl
from jax.experimental.pallas import tpu as pltpu

F32 = jnp.float32
BF16 = jnp.bfloat16

D_MODEL = 1024
HEAD_DIM = 64
MOBA_HEADS = 8
MOBA_BLOCK = 256
MOBA_TOPK = 3
SWA_Q_HEADS = 8
SWA_KV_HEADS = 2
SWA_WINDOW = 128
XATTN_HEADS = 4
XATTN_HEAD_DIM = 128
N_GROUPS = 4
EXPERTS_PER_GROUP = 8
D_EXPERT = 256
RMS_EPS = 1e-6
NEG = -1e30
LANES = 128
VMEM_LIMIT = 56 * 1024 * 1024

IN_TILE = 512
MIX_TILE = 512
MIX_SLAB = 256
MOE_TILE = 1024
MOE_CHUNK = LANES
MEM_TILE = 512


def _rms(x, g):
    return x * lax.rsqrt(jnp.mean(x * x, axis=-1, keepdims=True) + RMS_EPS) * g


def _dot(a, b):
    return jnp.dot(a, b, preferred_element_type=F32)


def _dot_nt(a, b):
    return lax.dot_general(a, b, (((1,), (1,)), ((), ())), preferred_element_type=F32)


_IN_WIDTHS = (512, 512, 512, 512, 256, 256, 1024, 1024)


def _in_proj_kernel(x_ref, g_ref, w_ref, *out_refs):
    h = _rms(x_ref[...], g_ref[...]).astype(BF16)
    c0 = 0
    for o_ref, width in zip(out_refs, _IN_WIDTHS):
        o_ref[...] = _dot(h, w_ref[:, c0:c0 + width]).astype(o_ref.dtype)
        c0 += width


def _in_proj(x2d, g, w_all):
    n = x2d.shape[0]
    tm = IN_TILE
    wtot = w_all.shape[1]
    return pl.pallas_call(
        _in_proj_kernel,
        grid=(n // tm,),
        in_specs=[
            pl.BlockSpec((tm, D_MODEL), lambda i: (i, 0)),
            pl.BlockSpec((1, D_MODEL), lambda i: (0, 0)),
            pl.BlockSpec((D_MODEL, wtot), lambda i: (0, 0)),
        ],
        out_specs=[pl.BlockSpec((tm, w), lambda i: (i, 0)) for w in _IN_WIDTHS],
        out_shape=[jax.ShapeDtypeStruct((n, w), BF16) for w in _IN_WIDTHS],
        compiler_params=pltpu.CompilerParams(
            dimension_semantics=("parallel",), vmem_limit_bytes=VMEM_LIMIT),
        name="in_proj",
    )(x2d, g, w_all)


_OFF_LANE = 8


def _moba_constants(t_len):
    nb = t_len // MOBA_BLOCK
    r = np.arange(t_len)
    kfeat = np.zeros((t_len, LANES), np.float32)
    kfeat[r, r // MOBA_BLOCK] = 1.0
    kfeat[:, _OFF_LANE] = r % MOBA_BLOCK
    c = np.arange(nb * nb)
    pairs = np.zeros((2 * LANES, t_len), np.float32)
    pairs[c[:, None], r[None, :]] = ((c % nb)[:, None] == (r // MOBA_BLOCK)[None, :])
    pairs[LANES + c[:, None], r[None, :]] = ((c // nb)[:, None] == (r // MOBA_BLOCK)[None, :])
    place = np.zeros((nb, LANES, LANES), np.float32)
    for i in range(nb):
        for m in range(i):
            for n in range(nb):
                if m != n:
                    place[i, nb * m + n, n] = 1.0
    return (jnp.asarray(kfeat, BF16), jnp.asarray(pairs, BF16), jnp.asarray(place, BF16))


def _moba_kernel(slopes_ref, q_ref, k_ref, v_ref, kfeat_ref, pairs_ref, place_ref, o_ref,
                 kaug_ref, vext_ref):
    hp = pl.program_id(1)
    t_len = q_ref.shape[0]
    nb = t_len // MOBA_BLOCK
    bs = MOBA_BLOCK

    k = k_ref[...]
    kaug_ref[:, 0:LANES] = k
    kaug_ref[:, LANES:2 * LANES] = kfeat_ref[...]
    vext_ref[:, 0:LANES] = v_ref[...]
    vext_ref[:, LANES:2 * LANES] = jnp.ones((t_len, LANES), BF16)
    kmean_t = _dot(pairs_ref[...], k) * (1.0 / bs)
    km = jnp.concatenate([kmean_t[0:LANES].T, kmean_t[LANES:2 * LANES].T], axis=1).astype(BF16)

    lane = lax.broadcasted_iota(jnp.int32, (bs, LANES), 1)
    lane2 = lax.broadcasted_iota(jnp.int32, (2 * bs, LANES), 1)
    rival_first = (lane2 // nb) < (lane2 % nb)
    s_row = lax.broadcasted_iota(jnp.int32, (2 * bs, bs), 0)
    s_col = lax.broadcasted_iota(jnp.int32, (2 * bs, bs), 1)
    causal = s_col <= (s_row % bs)
    slopes = (slopes_ref[2 * hp], slopes_ref[2 * hp + 1])
    scale = jnp.asarray(HEAD_DIM ** -0.5, BF16)

    def q_block(i):
        q = q_ref[i * bs:(i + 1) * bs, :]
        zero = jnp.zeros_like(q)
        qs = jnp.concatenate([jnp.where(lane < 64, q, zero) * scale,
                              jnp.where(lane >= 64, q, zero) * scale], axis=0)
        gates = _dot(qs, km)
        g_n, g_m = gates[:, 0:LANES], gates[:, LANES:2 * LANES]
        beats = (g_m > g_n) | (rival_first & (g_m == g_n))
        rank = _dot(jnp.where(beats, 1.0, 0.0).astype(BF16), place_ref[i])
        chosen = (rank < MOBA_TOPK - 0.5) & (lane2 < i)
        feats = []
        for hh in range(2):
            picked = (slopes[hh] * bs) * (lane - i).astype(F32)
            other = jnp.where(lane == _OFF_LANE, slopes[hh], jnp.where(lane < i, NEG, 0.0))
            feats.append(jnp.where(chosen[hh * bs:(hh + 1) * bs], picked, other))
        q_aug = jnp.concatenate([qs, jnp.concatenate(feats, axis=0).astype(BF16)], axis=1)
        yield

        m_i = l_i = acc = None
        for j in [i] + list(range(i)):
            if j != i:
                yield
            s = _dot_nt(q_aug, kaug_ref[j * bs:(j + 1) * bs, :])
            v_j = vext_ref[j * bs:(j + 1) * bs, :]
            if j == i:
                s = jnp.where(causal, s, NEG)
                m_i = jnp.max(s, axis=1, keepdims=True)
                p = jnp.exp(s - m_i)
                acc = _dot(p.astype(BF16), v_j)
            else:
                m_n = jnp.maximum(m_i, jnp.max(s, axis=1, keepdims=True))
                alpha = jnp.exp(m_i - m_n)
                p = jnp.exp(s - m_n)
                acc = alpha * acc + _dot(p.astype(BF16), v_j)
                m_i = m_n
        o = acc[:, 0:LANES] / acc[:, LANES:2 * LANES]
        o_ref[i * bs:(i + 1) * bs, :] = jnp.where(lane < 64, o[0:bs], o[bs:2 * bs]).astype(o_ref.dtype)

    assert nb % 4 == 0
    for first in range(0, nb // 2, 2):
        live = [q_block(nb - 1 - first), q_block(nb - 2 - first), q_block(first + 1), q_block(first)]
        while live:
            for blk in list(live):
                if next(blk, "done") == "done":
                    live.remove(blk)


def _moba(slopes, q3, k3, v3):
    b, t, w = q3.shape
    kfeat, pairs, place = _moba_constants(t)
    spec = pl.BlockSpec((None, t, LANES), lambda bi, hp: (bi, 0, hp))
    const = lambda arr: pl.BlockSpec(arr.shape, lambda bi, hp: (0,) * arr.ndim)
    return pl.pallas_call(
        _moba_kernel,
        grid=(b, w // LANES),
        in_specs=[pl.BlockSpec(memory_space=pltpu.SMEM), spec, spec, spec,
                  const(kfeat), const(pairs), const(place)],
        out_specs=spec,
        out_shape=jax.ShapeDtypeStruct((b, t, w), BF16),
        scratch_shapes=[pltpu.VMEM((t, 2 * LANES), BF16), pltpu.VMEM((t, 2 * LANES), BF16)],
        compiler_params=pltpu.CompilerParams(
            dimension_semantics=("parallel", "arbitrary"), vmem_limit_bytes=VMEM_LIMIT),
        name="moba",
    )(slopes, q3, k3, v3, kfeat, pairs, place)


def _swa_kernel(slopes_ref, sinks_ref, q_ref, kd_ref, vd_ref, o_ref, kp_ref, vp_ref, bias_ref):
    w = SWA_WINDOW
    t_len = q_ref.shape[0]
    kp_ref[0:w, :] = jnp.zeros((w, kp_ref.shape[1]), BF16)
    kp_ref[w:, :] = kd_ref[...]
    for kvh in range(SWA_KV_HEADS):
        vp_ref[0:w, 2 * kvh * LANES:(2 * kvh + 1) * LANES] = jnp.zeros((w, LANES), BF16)
        vp_ref[w:, 2 * kvh * LANES:(2 * kvh + 1) * LANES] = vd_ref[:, kvh * LANES:(kvh + 1) * LANES]
        vp_ref[:, (2 * kvh + 1) * LANES:(2 * kvh + 2) * LANES] = jnp.ones((t_len + w, LANES), BF16)

    r = lax.broadcasted_iota(jnp.int32, (w, 2 * w), 0)
    c = lax.broadcasted_iota(jnp.int32, (w, 2 * w), 1)
    rel = w + r - c
    in_window = (rel >= 0) & (rel < w)
    rel_f = rel.astype(F32)
    for h in range(SWA_Q_HEADS):
        bias_ref[h] = jnp.where(in_window, -slopes_ref[h] * rel_f, NEG)
    lane_q = lax.broadcasted_iota(jnp.int32, (w, LANES), 1)
    scale = jnp.asarray(HEAD_DIM ** -0.5, BF16)

    for n in range(t_len // w):
        qs = n * w
        for slab in range(SWA_Q_HEADS // 2):
            kvh = slab // (SWA_Q_HEADS // SWA_KV_HEADS // 2)
            q = q_ref[qs:qs + w, slab * LANES:(slab + 1) * LANES]
            kk = kp_ref[qs:qs + 2 * w, kvh * LANES:(kvh + 1) * LANES]
            vv = vp_ref[qs:qs + 2 * w, 2 * kvh * LANES:(2 * kvh + 2) * LANES]
            outs = []
            for half in range(2):
                h = 2 * slab + half
                head_lane = (lane_q >= 64 * half) & (lane_q < 64 * half + 64)
                qh = jnp.where(head_lane, q, jnp.zeros_like(q)) * scale
                s = _dot_nt(qh, kk) + bias_ref[h]
                if n == 0:
                    s = jnp.where(c >= w, s, NEG)
                sink = sinks_ref[h]
                m = jnp.maximum(jnp.max(s, axis=1, keepdims=True), sink)
                p = jnp.exp(s - m)
                pv = _dot(p.astype(BF16), vv)
                outs.append(pv[:, 0:LANES] / (pv[:, LANES:2 * LANES] + jnp.exp(sink - m)))
            o_ref[qs:qs + w, slab * LANES:(slab + 1) * LANES] = (
                jnp.where(lane_q < 64, outs[0], outs[1]).astype(o_ref.dtype))


def _swa(slopes, sinks, q3, kd3, vd3):
    b, t, wq = q3.shape
    wkv = kd3.shape[2]
    smem = pl.BlockSpec(memory_space=pltpu.SMEM)
    return pl.pallas_call(
        _swa_kernel,
        grid=(b,),
        in_specs=[smem, smem,
                  pl.BlockSpec((None, t, wq), lambda bi: (bi, 0, 0)),
                  pl.BlockSpec((None, t, wkv), lambda bi: (bi, 0, 0)),
                  pl.BlockSpec((None, t, wkv), lambda bi: (bi, 0, 0))],
        out_specs=pl.BlockSpec((None, t, wq), lambda bi: (bi, 0, 0)),
        out_shape=jax.ShapeDtypeStruct((b, t, wq), BF16),
        scratch_shapes=[pltpu.VMEM((t + SWA_WINDOW, wkv), BF16),
                        pltpu.VMEM((t + SWA_WINDOW, 2 * wkv), BF16),
                        pltpu.VMEM((SWA_Q_HEADS, SWA_WINDOW, 2 * SWA_WINDOW), F32)],
        compiler_params=pltpu.CompilerParams(
            dimension_semantics=("parallel",), vmem_limit_bytes=VMEM_LIMIT),
        name="swa",
    )(slopes, sinks, q3, kd3, vd3)


def _mem_kv_kernel(m_ref, g_ref, w_ref, o_ref):
    o_ref[...] = _dot(_rms(m_ref[...], g_ref[...]).astype(BF16), w_ref[...]).astype(o_ref.dtype)


def _mem_kv(mem2d, g, w_kv):
    n = mem2d.shape[0]
    tm = MEM_TILE
    wout = w_kv.shape[1]
    return pl.pallas_call(
        _mem_kv_kernel,
        grid=(n // tm,),
        in_specs=[pl.BlockSpec((tm, D_MODEL), lambda i: (i, 0)),
                  pl.BlockSpec((1, D_MODEL), lambda i: (0, 0)),
                  pl.BlockSpec((D_MODEL, wout), lambda i: (0, 0))],
        out_specs=pl.BlockSpec((tm, wout), lambda i: (i, 0)),
        out_shape=jax.ShapeDtypeStruct((n, wout), BF16),
        compiler_params=pltpu.CompilerParams(
            dimension_semantics=("parallel",), vmem_limit_bytes=VMEM_LIMIT),
        name="mem_kv",
    )(mem2d, g, w_kv)


_GROUP_LANE0 = N_GROUPS * EXPERTS_PER_GROUP
_LOW = -3.0e38


def _route(logits):
    lane = lax.broadcasted_iota(jnp.int32, logits.shape, 1)
    lane_f = lane.astype(F32)
    is_g = (lane >= _GROUP_LANE0) & (lane < _GROUP_LANE0 + N_GROUPS)
    gl = jnp.where(is_g, logits, _LOW)
    gmax = jnp.max(gl, axis=1, keepdims=True)
    g_p = 1.0 / jnp.sum(jnp.exp(gl - gmax), axis=1, keepdims=True)
    g_idx = jnp.min(jnp.where(gl == gmax, lane_f - _GROUP_LANE0, 1e9), axis=1, keepdims=True)
    in_group = (lane < _GROUP_LANE0) & ((lane // EXPERTS_PER_GROUP).astype(F32) == g_idx)
    el = jnp.where(in_group, logits, _LOW)
    m1 = jnp.max(el, axis=1, keepdims=True)
    i1 = jnp.min(jnp.where(el == m1, lane_f, 1e9), axis=1, keepdims=True)
    el2 = jnp.where(lane_f == i1, _LOW, el)
    m2 = jnp.max(el2, axis=1, keepdims=True)
    i2 = jnp.min(jnp.where(el2 == m2, lane_f, 1e9), axis=1, keepdims=True)
    t = jnp.exp(m2 - m1)
    w1 = g_p / (1.0 + t)
    w2 = w1 * t
    return jnp.where(lane_f == i1, w1, jnp.where(lane_f == i2, w2, 0.0))


def _mix_kernel(x_ref, a_ref, b_ref, ga_ref, gb_ref, kv_ref,
                wa_ref, wb_ref, wo_ref, g2_ref, wq_ref, wxo_ref, g3_ref,
                wrh_ref, wrl_ref, br_ref,
                x2_ref, h3_ref, comb_ref):
    slabs = [_mix_rows(slice(r0, r0 + MIX_SLAB), x_ref, a_ref, b_ref, ga_ref, gb_ref, kv_ref,
                       wa_ref, wb_ref, wo_ref, g2_ref, wq_ref, wxo_ref, g3_ref,
                       wrh_ref, wrl_ref, br_ref, x2_ref, h3_ref, comb_ref)
             for r0 in range(0, x_ref.shape[0], MIX_SLAB)]
    for _ in range(_MIX_STAGES):
        for slab in slabs:
            next(slab)


_MIX_STAGES = 8


def _mix_rows(rows, x_ref, a_ref, b_ref, ga_ref, gb_ref, kv_ref,
              wa_ref, wb_ref, wo_ref, g2_ref, wq_ref, wxo_ref, g3_ref,
              wrh_ref, wrl_ref, br_ref, x2_ref, h3_ref, comb_ref):
    pa = _dot(a_ref[rows, :], wa_ref[...])
    pb = _dot(b_ref[rows, :], wb_ref[...])
    yield
    merged = (jax.nn.sigmoid(ga_ref[rows, :].astype(F32)) * pa
              + jax.nn.sigmoid(gb_ref[rows, :].astype(F32)) * pb)
    x1 = x_ref[rows, :] + _dot(merged.astype(BF16), wo_ref[...])
    yield

    q = _dot(_rms(x1, g2_ref[...]).astype(BF16), wq_ref[...]).astype(BF16)
    yield
    width = XATTN_HEADS * XATTN_HEAD_DIM
    heads = []
    for hd in range(XATTN_HEADS):
        lo = hd * XATTN_HEAD_DIM
        kx = kv_ref[:, lo:lo + XATTN_HEAD_DIM]
        vx = kv_ref[:, width + lo:width + lo + XATTN_HEAD_DIM]
        s = _dot_nt(q[:, lo:lo + XATTN_HEAD_DIM], kx) * (XATTN_HEAD_DIM ** -0.5)
        m = jnp.max(s, axis=1, keepdims=True)
        p = jnp.exp(s - m)
        l = jnp.sum(p, axis=1, keepdims=True)
        heads.append((_dot(p.astype(BF16), vx) / l).astype(BF16))
        if hd % 2 == 1:
            yield
    x2 = x1 + _dot(jnp.concatenate(heads, axis=1), wxo_ref[...])
    x2_ref[rows, :] = x2
    yield

    h3 = _rms(x2, g3_ref[...])
    h3_hi = h3.astype(BF16)
    h3_lo = (h3 - h3_hi.astype(F32)).astype(BF16)
    h3_ref[rows, :] = h3_hi
    hi_terms = _dot(h3_hi, jnp.concatenate([wrh_ref[...], wrl_ref[...]], axis=1))
    logits = (hi_terms[:, 0:LANES]
              + (hi_terms[:, LANES:2 * LANES] + _dot(h3_lo, wrh_ref[...]))
              + br_ref[...])
    yield
    comb_ref[rows, :] = _route(logits)
    yield


def _mix(x2d, a, b, ga, gb, kv3, wa, wb, wo, g2, wq, wxo, g3, wrh, wrl, br, seq_len):
    n = x2d.shape[0]
    tm = MIX_TILE
    tiles_per_seq = seq_len // tm
    row = lambda w: pl.BlockSpec((tm, w), lambda i: (i, 0))
    full = lambda arr: pl.BlockSpec(arr.shape, lambda i: (0,) * arr.ndim)
    mem_len, kvw = kv3.shape[1], kv3.shape[2]
    return pl.pallas_call(
        _mix_kernel,
        grid=(n // tm,),
        in_specs=[row(D_MODEL), row(a.shape[1]), row(b.shape[1]), row(D_MODEL), row(D_MODEL),
                  pl.BlockSpec((None, mem_len, kvw), lambda i: (i // tiles_per_seq, 0, 0)),
                  full(wa), full(wb), full(wo), full(g2), full(wq), full(wxo), full(g3),
                  full(wrh), full(wrl), full(br)],
        out_specs=[row(D_MODEL), row(D_MODEL), row(LANES)],
        out_shape=[jax.ShapeDtypeStruct((n, D_MODEL), F32),
                   jax.ShapeDtypeStruct((n, D_MODEL), BF16),
                   jax.ShapeDtypeStruct((n, LANES), F32)],
        compiler_params=pltpu.CompilerParams(
            dimension_semantics=("parallel",), vmem_limit_bytes=VMEM_LIMIT),
        name="mix",
    )(x2d, a, b, ga, gb, kv3, wa, wb, wo, g2, wq, wxo, g3, wrh, wrl, br)


def _dot_tn(a, b):
    return lax.dot_general(a, b, (((0,), (0,)), ((), ())), preferred_element_type=F32)


def _moe_kernel(h_ref, comb_ref, x_ref, wg_ref, wu_ref, wd_ref, gf_ref, o_ref,
                slot_ref, terms_ref, cnt_ref, hid_ref):
    g = pl.program_id(1)
    tm = h_ref.shape[0]
    ch = MOE_CHUNK

    lane = lax.broadcasted_iota(jnp.int32, (tm, LANES), 1)
    lane_f = lane.astype(F32)
    lo = g * EXPERTS_PER_GROUP
    span = LANES // N_GROUPS

    @pl.when(g == 0)
    def _():
        o_ref[...] = jnp.zeros(o_ref.shape, o_ref.dtype)
        r = lax.broadcasted_iota(jnp.int32, (tm, tm), 0)
        c = lax.broadcasted_iota(jnp.int32, (tm, tm), 1)
        tri = jnp.where(c < r, 1.0, 0.0).astype(BF16)
        comb = comb_ref[...]
        rr = lax.broadcasted_iota(jnp.int32, (LANES, LANES), 0)
        cc = lax.broadcasted_iota(jnp.int32, (LANES, LANES), 1)
        to_group = jnp.where((rr < _GROUP_LANE0) & (rr // EXPERTS_PER_GROUP == cc // span), 1.0, 0.0)
        routed = jnp.where((comb > 0.0) & (lane < _GROUP_LANE0), 1.0, 0.0).astype(BF16)
        member = _dot(routed, to_group.astype(BF16)) > 0.5
        rank = _dot(tri, jnp.where(member, 1.0, 0.0).astype(BF16))
        slot = jnp.where(member, rank + 1.0, 0.0)
        hi = jnp.floor(slot * (1.0 / 32.0))
        slot_ref[...] = jnp.concatenate([hi, slot - 32.0 * hi], axis=1).astype(BF16)
        for q in range(N_GROUPS):
            cnt_ref[q] = jnp.max(slot[:, q * span:(q + 1) * span]).astype(jnp.int32)
        c1 = comb.astype(BF16)
        r1 = comb - c1.astype(F32)
        c2 = r1.astype(BF16)
        c3 = (r1 - c2.astype(F32)).astype(BF16)
        terms_ref[...] = jnp.concatenate([c1, c2, c3], axis=1)

    kk = lax.broadcasted_iota(jnp.int32, (2 * LANES, LANES), 0)
    pick = jnp.where(kk == g * span, 32.0, jnp.where(kk == LANES + g * span, 1.0, 0.0)).astype(BF16)
    slot_g = _dot(slot_ref[...], pick)
    n_chunks = (cnt_ref[g] + (ch - 1)) // ch

    def chunk(ci, carry):
        base = (ci * ch + 1).astype(F32)
        onehot = jnp.where(slot_g == base + lane_f, 1.0, 0.0).astype(BF16)
        hc = _dot_tn(onehot, h_ref[...]).astype(BF16)
        ct = _dot_tn(onehot, terms_ref[...])
        cc = (ct[:, 0:LANES] + ct[:, LANES:2 * LANES]) + ct[:, 2 * LANES:3 * LANES]
        lane_c = lax.broadcasted_iota(jnp.int32, cc.shape, 1)
        for e in range(EXPERTS_PER_GROUP):
            gate = _dot(hc, wg_ref[e])
            up = _dot(hc, wu_ref[e])
            c = jnp.sum(jnp.where(lane_c == lo + e, cc, 0.0), axis=1, keepdims=True)
            hid_ref[:, e * D_EXPERT:(e + 1) * D_EXPERT] = (
                gate * jax.nn.sigmoid(gate) * up * c).astype(BF16)
        y = _dot(hid_ref[...], wd_ref[...])
        y_hi = y.astype(BF16)
        y_lo = (y - y_hi.astype(F32)).astype(BF16)
        o_ref[...] += _dot(jnp.concatenate([onehot, onehot], axis=1),
                           jnp.concatenate([y_hi, y_lo], axis=0))
        return carry

    lax.fori_loop(0, n_chunks, chunk, 0)

    @pl.when(g == N_GROUPS - 1)
    def _():
        o_ref[...] = _rms(x_ref[...] + o_ref[...], gf_ref[...])


def _moe(h3, comb, x2, wg, wu, wd, gf):
    n = h3.shape[0]
    tm = MOE_TILE
    return pl.pallas_call(
        _moe_kernel,
        grid=(n // tm, N_GROUPS),
        in_specs=[pl.BlockSpec((tm, D_MODEL), lambda i, g: (i, 0)),
                  pl.BlockSpec((tm, LANES), lambda i, g: (i, 0)),
                  pl.BlockSpec((tm, D_MODEL), lambda i, g: (i, 0)),
                  pl.BlockSpec((None, EXPERTS_PER_GROUP, D_MODEL, D_EXPERT), lambda i, g: (g, 0, 0, 0)),
                  pl.BlockSpec((None, EXPERTS_PER_GROUP, D_MODEL, D_EXPERT), lambda i, g: (g, 0, 0, 0)),
                  pl.BlockSpec((None, EXPERTS_PER_GROUP * D_EXPERT, D_MODEL), lambda i, g: (g, 0, 0)),
                  pl.BlockSpec((1, D_MODEL), lambda i, g: (0, 0))],
        out_specs=pl.BlockSpec((tm, D_MODEL), lambda i, g: (i, 0)),
        out_shape=jax.ShapeDtypeStruct((n, D_MODEL), F32),
        scratch_shapes=[pltpu.VMEM((tm, 2 * LANES), BF16),
                        pltpu.VMEM((tm, 3 * LANES), BF16),
                        pltpu.SMEM((N_GROUPS,), jnp.int32),
                        pltpu.VMEM((MOE_CHUNK, EXPERTS_PER_GROUP * D_EXPERT), BF16)],
        compiler_params=pltpu.CompilerParams(
            dimension_semantics=("parallel", "arbitrary"), vmem_limit_bytes=VMEM_LIMIT),
        name="moe",
    )(h3, comb, x2, wg, wu, wd, gf)


def _alibi_slopes(n_heads):
    return jnp.asarray([2.0 ** (-8.0 * (i + 1) / n_heads) for i in range(n_heads)], dtype=F32)


def _split_bf16(w):
    hi = w.astype(BF16)
    return hi, (w - hi.astype(F32)).astype(BF16)


def kernel(x, mem, mix_norm_g, w_in, moba_out_w, swa_out_w, swa_sinks, mix_out_w, xattn_norm_g, mem_norm_g, xattn_wq, xattn_wkv, xattn_wo, ffn_norm_g, router_group_w, router_group_b, router_expert_w, router_expert_b, expert_w_gate, expert_w_up, expert_w_down, final_norm_g):
    bsz, t, d = x.shape
    n = bsz * t
    assert w_in.shape[0] == 1, "the final RMSNorm is fused into the (single) layer's MoE kernel"
    l = 0
    x2d = x.reshape(n, d)
    mem2d = mem.reshape(-1, d)
    mem_len = mem.shape[1]
    moba_slopes = _alibi_slopes(MOBA_HEADS)
    swa_slopes = _alibi_slopes(SWA_Q_HEADS)

    w = w_in[l]
    c = 3 * MOBA_HEADS * HEAD_DIM + SWA_Q_HEADS * HEAD_DIM
    kvw = SWA_KV_HEADS * HEAD_DIM
    dup = lambda m: jnp.concatenate(
        [m[:, j * HEAD_DIM:(j + 1) * HEAD_DIM] for j in range(SWA_KV_HEADS) for _ in range(2)], axis=1)
    w_all = jnp.concatenate(
        [w[:, :c], dup(w[:, c:c + kvw]), dup(w[:, c + kvw:c + 2 * kvw]), w[:, c + 2 * kvw:]],
        axis=1).astype(BF16)
    qa, ka, va, qb, kd, vd, ga, gb = _in_proj(x2d, mix_norm_g[l].reshape(1, d), w_all)

    r3 = lambda arr: arr.reshape(bsz, t, arr.shape[1])
    a = _moba(moba_slopes, r3(qa), r3(ka), r3(va)).reshape(n, -1)
    b = _swa(swa_slopes, swa_sinks[l], r3(qb), r3(kd), r3(vd)).reshape(n, -1)

    kv = _mem_kv(mem2d, mem_norm_g[l].reshape(1, d), xattn_wkv[l].astype(BF16))
    kv3 = kv.reshape(bsz, mem_len, kv.shape[1])

    pad = LANES - _GROUP_LANE0 - N_GROUPS
    w_r = jnp.concatenate([router_expert_w[l], router_group_w[l], jnp.zeros((d, pad), F32)], axis=1)
    b_r = jnp.concatenate([router_expert_b[l], router_group_b[l], jnp.zeros((pad,), F32)]).reshape(1, LANES)
    wrh, wrl = _split_bf16(w_r)
    x2, h3, comb = _mix(
        x2d, a, b, ga, gb, kv3,
        moba_out_w[l].astype(BF16), swa_out_w[l].astype(BF16), mix_out_w[l].astype(BF16),
        xattn_norm_g[l].reshape(1, d), xattn_wq[l].astype(BF16), xattn_wo[l].astype(BF16),
        ffn_norm_g[l].reshape(1, d), wrh, wrl, b_r, t)

    wd = expert_w_down[l].astype(BF16).reshape(N_GROUPS, EXPERTS_PER_GROUP * D_EXPERT, d)
    out = _moe(h3, comb, x2, expert_w_gate[l].astype(BF16), expert_w_up[l].astype(BF16), wd,
               final_norm_g.reshape(1, d))
    return out.reshape(bsz, t, d)
```
